```python
import math
import jax, jax.numpy as jnp
from jax import lax
import numpy as np

D_MODEL = 1024
BATCH = 32
SEQ = 2048
DEPTH = 2
DEC_BATCH = 128
DEC_SEQ = 8
PAST_LEN = 16384
PAGE_SIZE = 128

MLA_HEADS = 8
MLA_Q_LORA = 256
MLA_KV_LORA = 128
MLA_NOPE = 64
MLA_ROPE = 32
MLA_V = 64
MLA_SCALE = (MLA_NOPE + MLA_ROPE) ** -0.5
DSA_HEADS = 8
DSA_KV_HEADS = 2
DSA_HEAD_DIM = 64
DSA_SCALE = DSA_HEAD_DIM ** -0.5
IDX_HEADS = 4
IDX_DIM = 64
IDX_SCALE = (IDX_HEADS * IDX_DIM) ** -0.5
IDX_TOPK_MAX = 256
GLA_HEADS = 4
GLA_DK = 64
GLA_DV = 128
GLA_GATE_RANK = 16
GLA_TAU = 16.0
GLA_CHUNK = 16
D_FF = 2816
N_BRANCH = 3
BRANCH_W = MLA_HEADS * MLA_V
FFN_RESID = 0.5
Q_BLOCK = 128
ROPE_THETA = 10000.0
EPS = 1e-6

IN_SPLITS = (
    ('mla_cq', MLA_Q_LORA),
    ('mla_ckv', MLA_KV_LORA),
    ('mla_krope', MLA_ROPE),
    ('dsa_q', DSA_HEADS * DSA_HEAD_DIM),
    ('dsa_k', DSA_KV_HEADS * DSA_HEAD_DIM),
    ('dsa_v', DSA_KV_HEADS * DSA_HEAD_DIM),
    ('idx_q', IDX_HEADS * IDX_DIM),
    ('idx_k', IDX_DIM),
    ('idx_w', IDX_HEADS),
    ('gla_q', GLA_HEADS * GLA_DK),
    ('gla_k', GLA_HEADS * GLA_DK),
    ('gla_v', GLA_HEADS * GLA_DV),
    ('gla_a', GLA_GATE_RANK),
    ('gla_g', GLA_HEADS * GLA_DV),
    ('merge', N_BRANCH * D_MODEL),
)
D_IN = sum(n for _, n in IN_SPLITS)

kernel_name = 'hybrid_mla_dsa_gla_step'


def rmsnorm(x, g):
    xf = x.astype(jnp.float32)
    y = xf * lax.rsqrt(jnp.mean(xf * xf, axis=-1, keepdims=True) + EPS)
    return (y * g.astype(jnp.float32)).astype(x.dtype)


def rope(x, pos):
    half = x.shape[-1] // 2
    inv = ROPE_THETA ** (-jnp.arange(half, dtype=jnp.float32) / half)
    ang = pos.astype(jnp.float32)[:, None] * inv[None, :]
    cos = jnp.cos(ang)[None, :, None, :]
    sin = jnp.sin(ang)[None, :, None, :]
    xf = x.astype(jnp.float32)
    x1, x2 = xf[..., :half], xf[..., half:]
    return jnp.concatenate([x1 * cos - x2 * sin, x2 * cos + x1 * sin], axis=-1).astype(x.dtype)


def swiglu(h, w_gate, w_up, w_down):
    return (jax.nn.silu(h @ w_gate) * (h @ w_up)) @ w_down


def gather_rows(a, idx):
    return jax.vmap(lambda a_n, i_n: a_n[i_n])(a, idx)


def paged_rows(pool, page_table):
    g = pool[page_table]
    return g.reshape((g.shape[0], g.shape[1] * g.shape[2]) + g.shape[3:])


def gather_selected_sample(pool, new_rows, page_table, idx, past_len):
    flat = pool.reshape((-1,) + pool.shape[2:])
    ip = jnp.minimum(idx, past_len - 1)
    page = jax.vmap(lambda pt, i: pt[i])(page_table, ip // PAGE_SIZE)
    rows_past = flat[page * PAGE_SIZE + ip % PAGE_SIZE]
    rows_new = gather_rows(new_rows, jnp.clip(idx - past_len, 0, new_rows.shape[1] - 1))
    is_past = (idx < past_len).reshape(idx.shape + (1,) * (rows_past.ndim - idx.ndim))
    return jnp.where(is_past, rows_past, rows_new)


def mla_attend(q_abs, q_rope, c_kv, k_rope, q_pos, k_pos, w_uv):
    s = (jnp.einsum('nthr,nsr->nhts', q_abs, c_kv)
         + jnp.einsum('nthd,nsd->nhts', q_rope, k_rope)).astype(jnp.float32) * MLA_SCALE
    s = jnp.where((k_pos[None, :] <= q_pos[:, None])[None, None], s, -jnp.inf)
    pr = jax.nn.softmax(s, axis=-1).astype(c_kv.dtype)
    o_lat = jnp.einsum('nhts,nsr->nthr', pr, c_kv)
    o = jnp.einsum('nthr,rhd->nthd', o_lat, w_uv)
    return o.reshape(o.shape[0], o.shape[1], MLA_HEADS * MLA_V)


def index_select(q_i, w_i, k_i, q_pos, topk):
    dots = jnp.einsum('nthd,nsd->nths', q_i, k_i).astype(jnp.float32)
    score = jnp.einsum('nth,nths->nts', w_i.astype(jnp.float32) * IDX_SCALE, jax.nn.relu(dots))
    causal = jnp.arange(k_i.shape[1])[None, None, :] <= q_pos[None, :, None]
    score = jnp.where(causal, score, -jnp.inf)
    _, idx = lax.top_k(score, topk)
    valid = idx <= q_pos[None, :, None]
    return idx, valid


def sparse_attend(q, k_sel, v_sel, valid):
    n, t = q.shape[0], q.shape[1]
    qg = q.reshape(n, t, DSA_KV_HEADS, DSA_HEADS // DSA_KV_HEADS, DSA_HEAD_DIM)
    s = jnp.einsum('ntjgd,ntsjd->ntjgs', qg, k_sel).astype(jnp.float32) * DSA_SCALE
    s = jnp.where(valid[:, :, None, None, :], s, -jnp.inf)
    pr = jax.nn.softmax(s, axis=-1).astype(v_sel.dtype)
    o = jnp.einsum('ntjgs,ntsjd->ntjgd', pr, v_sel)
    return o.reshape(n, t, DSA_HEADS * DSA_HEAD_DIM)


def gla(q, k, v, log_a, s0):
    n, t, h, dk = q.shape
    dv = v.shape[-1]
    c = math.gcd(t, GLA_CHUNK)
    nc = t // c

    def chunks(a):
        return a.astype(jnp.float32).reshape(n, nc, c, h, a.shape[-1]).transpose(1, 0, 3, 2, 4)

    tri = jnp.tril(jnp.ones((c, c), dtype=bool))[None, None, :, :, None]

    def step(state, inp):
        qb, kb, vb, ab = inp
        b = jnp.cumsum(ab, axis=2)
        o_inter = jnp.einsum('nhtd,nhde->nhte', qb * jnp.exp(b), state)
        diff = b[:, :, :, None, :] - b[:, :, None, :, :]
        decay = jnp.exp(jnp.where(tri, diff, -jnp.inf))
        att = jnp.einsum('nhtd,nhsd,nhtsd->nhts', qb, kb, decay)
        o = o_inter + jnp.einsum('nhts,nhse->nhte', att, vb)
        b_last = b[:, :, -1:, :]
        state = (jnp.exp(b_last[:, :, 0, :])[..., None] * state
                 + jnp.einsum('nhsd,nhse->nhde', kb * jnp.exp(b_last - b), vb))
        return state, o

    s_fin, o = lax.scan(step, s0.astype(jnp.float32), (chunks(q), chunks(k), chunks(v), chunks(log_a)))
    o = o.transpose(1, 0, 3, 2, 4).reshape(n, t, h, dv)
    return o.astype(v.dtype), s_fin


def mixer_in(h, pos, lw):
    n, t = h.shape[0], h.shape[1]
    z = h @ lw['w_in']
    offs = [int(o) for o in np.cumsum([s for _, s in IN_SPLITS])[:-1]]
    p = dict(zip([nm for nm, _ in IN_SPLITS], jnp.split(z, offs, axis=-1)))
    cq = rmsnorm(p['mla_cq'], lw['g_q'])
    qh = (cq @ lw['w_uq']).reshape(n, t, MLA_HEADS, MLA_NOPE + MLA_ROPE)
    w_uk = lw['w_ukv'][..., :MLA_NOPE]
    q_abs = jnp.einsum('nthd,rhd->nthr', qh[..., :MLA_NOPE], w_uk)
    q_rope = rope(qh[..., MLA_NOPE:], pos)
    c_kv = rmsnorm(p['mla_ckv'], lw['g_kv'])
    k_rope = rope(p['mla_krope'][:, :, None, :], pos)[:, :, 0, :]
    q_b = rope(p['dsa_q'].reshape(n, t, DSA_HEADS, DSA_HEAD_DIM), pos)
    k_b = rope(p['dsa_k'].reshape(n, t, DSA_KV_HEADS, DSA_HEAD_DIM), pos)
    v_b = p['dsa_v'].reshape(n, t, DSA_KV_HEADS, DSA_HEAD_DIM)
    q_i = rope(p['idx_q'].reshape(n, t, IDX_HEADS, IDX_DIM), pos)
    k_i = rope(p['idx_k'][:, :, None, :], pos)[:, :, 0, :]
    q_c = p['gla_q'].reshape(n, t, GLA_HEADS, GLA_DK) * (GLA_DK ** -0.5)
    k_c = p['gla_k'].reshape(n, t, GLA_HEADS, GLA_DK)
    v_c = p['gla_v'].reshape(n, t, GLA_HEADS, GLA_DV)
    gate_logit = (p['gla_a'] @ lw['w_gla_a'] + lw['b_gla_a']).astype(jnp.float32)
    log_a = (jax.nn.log_sigmoid(gate_logit) / GLA_TAU).reshape(n, t, GLA_HEADS, GLA_DK)
    return dict(q_abs=q_abs, q_rope=q_rope, c_kv=c_kv, k_rope=k_rope,
                q_b=q_b, k_b=k_b, v_b=v_b, q_i=q_i, w_i=p['idx_w'], k_i=k_i,
                q_c=q_c, k_c=k_c, v_c=v_c, log_a=log_a, gla_g=p['gla_g'], merge=p['merge'])


def mixer_out(o_a, o_b, o_c, p, lw):
    n, t = o_a.shape[0], o_a.shape[1]
    o_c = rmsnorm(o_c, lw['g_gla']).reshape(n, t, GLA_HEADS * GLA_DV) * jax.nn.silu(p['gla_g'])
    br = jnp.stack([o_a, o_b, o_c], axis=2)
    y = jnp.einsum('ntbi,bid->ntbd', br, lw['w_branch'])
    gates = jax.nn.sigmoid(p['merge'].reshape(n, t, N_BRANCH, D_MODEL))
    return jnp.sum(gates * y, axis=2) @ lw['w_o']


def mixer_prompt(h, lw):
    n, s = h.shape[0], h.shape[1]
    pos = jnp.arange(s)
    p = mixer_in(h, pos, lw)
    w_uv = lw['w_ukv'][..., MLA_NOPE:]
    topk = min(IDX_TOPK_MAX, s // 4)

    def query_block(i):
        start = i * Q_BLOCK
        q_pos = start + jnp.arange(Q_BLOCK)
        sl = lambda a: lax.dynamic_slice_in_dim(a, start, Q_BLOCK, axis=1)
        o_a = mla_attend(sl(p['q_abs']), sl(p['q_rope']), p['c_kv'], p['k_rope'], q_pos, pos, w_uv)
        idx, valid = index_select(sl(p['q_i']), sl(p['w_i']), p['k_i'], q_pos, topk)
        o_b = sparse_attend(sl(p['q_b']), gather_rows(p['k_b'], idx), gather_rows(p['v_b'], idx), valid)
        return o_a, o_b

    o_a, o_b = lax.map(query_block, jnp.arange(s // Q_BLOCK))
    o_a = jnp.moveaxis(o_a, 0, 1).reshape(n, s, BRANCH_W)
    o_b = jnp.moveaxis(o_b, 0, 1).reshape(n, s, BRANCH_W)
    s0 = jnp.zeros((n, GLA_HEADS, GLA_DK, GLA_DV), jnp.float32)
    o_c, s_fin = gla(p['q_c'], p['k_c'], p['v_c'], p['log_a'], s0)
    y = mixer_out(o_a, o_b, o_c, p, lw)
    return y, (p['c_kv'], p['k_rope'], p['k_b'], p['v_b'], p['k_i'], s_fin.astype(h.dtype))


def mixer_sample(h, lw, lat_pool, krope_pool, k_pool, v_pool, ik_pool, s0, page_table):
    n, t = h.shape[0], h.shape[1]
    past = page_table.shape[1] * PAGE_SIZE
    pos = past + jnp.arange(t)
    k_pos = jnp.arange(past + t)
    p = mixer_in(h, pos, lw)
    w_uv = lw['w_ukv'][..., MLA_NOPE:]
    c_kv_all = jnp.concatenate([paged_rows(lat_pool, page_table), p['c_kv']], axis=1)
    k_rope_all = jnp.concatenate([paged_rows(krope_pool, page_table), p['k_rope']], axis=1)
    o_a = mla_attend(p['q_abs'], p['q_rope'], c_kv_all, k_rope_all, pos, k_pos, w_uv)
    k_i_all = jnp.concatenate([paged_rows(ik_pool, page_table), p['k_i']], axis=1)
    idx, valid = index_select(p['q_i'], p['w_i'], k_i_all, pos, min(IDX_TOPK_MAX, (past + t) // 4))
    k_sel = gather_selected_sample(k_pool, p['k_b'], page_table, idx, past)
    v_sel = gather_selected_sample(v_pool, p['v_b'], page_table, idx, past)
    o_b = sparse_attend(p['q_b'], k_sel, v_sel, valid)
    o_c, s_new = gla(p['q_c'], p['k_c'], p['v_c'], p['log_a'], s0)
    y = mixer_out(o_a, o_b, o_c, p, lw)
    return y, (p['c_kv'], p['k_rope'], p['k_b'], p['v_b'], p['k_i'], s_new.astype(s0.dtype))


def trunk_layer(x, c, lw, mixer_fn):
    n = x.shape[0]
    mod = (jax.nn.silu(c) @ lw['w_mod'] + lw['b_mod']).reshape(n, 3, 3, D_MODEL)[:, :, :, None, :]
    g = lw['norm_g']

    def pre(i, v):
        return rmsnorm(v, g[2 * i]) * (1 + mod[:, i, 1]) + mod[:, i, 0]

    def post(i, v):
        return mod[:, i, 2] * rmsnorm(v, g[2 * i + 1])

    x = x + FFN_RESID * post(0, swiglu(pre(0, x), lw['w_ff_gate'][0], lw['w_ff_up'][0], lw['w_ff_down'][0]))
    y, new_state = mixer_fn(pre(1, x))
    x = x + post(1, y)
    x = x + FFN_RESID * post(2, swiglu(pre(2, x), lw['w_ff_gate'][1], lw['w_ff_up'][1], lw['w_ff_down'][1]))
    return x, new_state


def setup_inputs(seed: int = 0) -> dict:
    key = jax.random.key(seed)
    ks = iter(jax.random.split(key, 40))

    def nrm(shape, s=1.0):
        return s * jax.random.normal(next(ks), shape, jnp.float32)

    def gain(shape):
        return 1.0 + nrm(shape, 0.02)

    n_pages = PAST_LEN // PAGE_SIZE
    n_phys = (DEC_BATCH * n_pages * 5) // 4
    page_table = jax.random.permutation(next(ks), n_phys)[: DEC_BATCH * n_pages]
    page_table = page_table.reshape(DEC_BATCH, n_pages).astype(jnp.int32)
    return {
        'x_prompt': nrm((BATCH, SEQ, D_MODEL)),
        'x_sample': nrm((DEC_BATCH, DEC_SEQ, D_MODEL)),
        'cache_mla_latent': nrm((DEPTH, n_phys, PAGE_SIZE, MLA_KV_LORA)),
        'cache_mla_krope': nrm((DEPTH, n_phys, PAGE_SIZE, MLA_ROPE)),
        'cache_dsa_k': nrm((DEPTH, n_phys, PAGE_SIZE, DSA_KV_HEADS, DSA_HEAD_DIM)),
        'cache_dsa_v': nrm((DEPTH, n_phys, PAGE_SIZE, DSA_KV_HEADS, DSA_HEAD_DIM)),
        'cache_dsa_idx_k': nrm((DEPTH, n_phys, PAGE_SIZE, IDX_DIM)),
        'state_gla': nrm((DEPTH, DEC_BATCH, GLA_HEADS, GLA_DK, GLA_DV), 0.5),
        'page_table': page_table,
        'c_prompt': nrm((BATCH, D_MODEL)),
        'c_sample': nrm((DEC_BATCH, D_MODEL)),
        'w_mod': nrm((DEPTH, D_MODEL, 9 * D_MODEL), 0.5 * D_MODEL ** -0.5),
        'b_mod': nrm((DEPTH, 9 * D_MODEL), 0.01),
        'norm_g': gain((DEPTH, 6, D_MODEL)),
        'w_ff_gate': nrm((DEPTH, 2, D_MODEL, D_FF), D_MODEL ** -0.5),
        'w_ff_up': nrm((DEPTH, 2, D_MODEL, D_FF), D_MODEL ** -0.5),
        'w_ff_down': nrm((DEPTH, 2, D_FF, D_MODEL), D_FF ** -0.5),
        'w_in': nrm((DEPTH, D_MODEL, D_IN), D_MODEL ** -0.5),
        'g_q': gain((DEPTH, MLA_Q_LORA)),
        'g_kv': gain((DEPTH, MLA_KV_LORA)),
        'w_uq': nrm((DEPTH, MLA_Q_LORA, MLA_HEADS * (MLA_NOPE + MLA_ROPE)), MLA_Q_LORA ** -0.5),
        'w_ukv': nrm((DEPTH, MLA_KV_LORA, MLA_HEADS, MLA_NOPE + MLA_V), MLA_KV_LORA ** -0.5),
        'w_gla_a': nrm((DEPTH, GLA_GATE_RANK, GLA_HEADS * GLA_DK), GLA_GATE_RANK ** -0.5),
        'b_gla_a': nrm((DEPTH, GLA_HEADS * GLA_DK), 0.1),
        'g_gla': gain((DEPTH, GLA_DV)),
        'w_branch': nrm((DEPTH, N_BRANCH, BRANCH_W, D_MODEL), BRANCH_W ** -0.5),
        'w_o': nrm((DEPTH, D_MODEL, D_MODEL), D_MODEL ** -0.5),
    }


def reference(x_prompt, x_sample, cache_mla_latent, cache_mla_krope, cache_dsa_k, cache_dsa_v,
              cache_dsa_idx_k, state_gla, page_table, c_prompt, c_sample, w_mod, b_mod, norm_g,
              w_ff_gate, w_ff_up, w_ff_down, w_in, g_q, g_kv, w_uq, w_ukv, w_gla_a, b_gla_a,
              g_gla, w_branch, w_o):
    xp, xs = x_prompt, x_sample
    new_p, new_s = [], []
    for l in range(DEPTH):
        lw = dict(w_mod=w_mod[l], b_mod=b_mod[l], norm_g=norm_g[l], w_ff_gate=w_ff_gate[l],
                  w_ff_up=w_ff_up[l], w_ff_down=w_ff_down[l], w_in=w_in[l], g_q=g_q[l], g_kv=g_kv[l],
                  w_uq=w_uq[l], w_ukv=w_ukv[l], w_gla_a=w_gla_a[l], b_gla_a=b_gla_a[l], g_gla=g_gla[l],
                  w_branch=w_branch[l], w_o=w_o[l])
        xp, st_p = trunk_layer(xp, c_prompt, lw, lambda h: mixer_prompt(h, lw))
        xs, st_s = trunk_layer(
            xs, c_sample, lw,
            lambda h: mixer_sample(h, lw, cache_mla_latent[l], cache_mla_krope[l], cache_dsa_k[l],
                                   cache_dsa_v[l], cache_dsa_idx_k[l], state_gla[l], page_table))
        new_p.append(st_p)
        new_s.append(st_s)
    lat_p, krope_p, k_p, v_p, ik_p, gla_p = [jnp.stack(a) for a in zip(*new_p)]
    lat_s, krope_s, k_s, v_s, ik_s, gla_s = [jnp.stack(a) for a in zip(*new_s)]
    return (xp, xs, lat_p, lat_s, krope_p, krope_s, k_p, k_s, v_p, v_s, ik_p, ik_s, gla_p, gla_s)
```

```python
import functools

import numpy as np
import jax
import jax.numpy as jnp
from jax import lax
from jax.experimental import pallas as pl
from jax.experimental.pallas import tpu as pltpu

F32 = jnp.float32
BF16 = jnp.bfloat16

D_MODEL = 1024
PAGE_SIZE = 128
MLA_HEADS = 8
MLA_Q_LORA = 256
MLA_KV_LORA = 128
MLA_NOPE = 64
MLA_ROPE = 32
MLA_V = 64
MLA_SCALE = (MLA_NOPE + MLA_ROPE) ** -0.5
DSA_HEADS = 8
DSA_KV_HEADS = 2
DSA_HEAD_DIM = 64
DSA_SCALE = DSA_HEAD_DIM ** -0.5
IDX_HEADS = 4
IDX_DIM = 64
IDX_SCALE = (IDX_HEADS * IDX_DIM) ** -0.5
IDX_TOPK_MAX = 256
GLA_HEADS = 4
GLA_DK = 64
GLA_DV = 128
GLA_GATE_RANK = 16
GLA_TAU = 16.0
GLA_CHUNK = 16
D_FF = 2816
N_BRANCH = 3
BRANCH_W = 512
FFN_RESID = 0.5
ROPE_THETA = 10000.0
EPS = 1e-6

LANES = 128
SUBLANES = 8
VMEM_LIMIT = 56 * 1024 * 1024
INT_MIN = -2 ** 31

_IN_SPLITS = (
    ('mla_cq', 256), ('mla_ckv', 128), ('mla_krope', 32), ('dsa_q', 512), ('dsa_k', 128), ('dsa_v', 128),
    ('idx_q', 256), ('idx_k', 64), ('idx_w', 4), ('gla_q', 256), ('gla_k', 256), ('gla_v', 512),
    ('gla_a', 16), ('gla_g', 512), ('merge', 3072),
)
_IN_OFF = {}
_o = 0
for _nm, _w in _IN_SPLITS:
    _IN_OFF[_nm] = (_o, _o + _w)
    _o += _w


def _cparams(*sem):
    return pltpu.CompilerParams(dimension_semantics=sem, vmem_limit_bytes=VMEM_LIMIT)


def _dot(a, b):
    return jnp.dot(a, b, preferred_element_type=F32)


def _dot_nt(a, b):
    return lax.dot_general(a, b, (((1,), (1,)), ((), ())), preferred_element_type=F32)


def _sigmoid(x):
    return 1.0 / (1.0 + jnp.exp(-x))


def _silu(x):
    return x * _sigmoid(x)


def _log_sigmoid(x):
    return -(jnp.maximum(-x, 0.0) + jnp.log(1.0 + jnp.exp(-jnp.abs(x))))


def _rms(x, g):
    return x * lax.rsqrt(jnp.mean(x * x, axis=-1, keepdims=True) + EPS) * g


def _rope(x, cos, sin_signed, half):
    w = x.shape[-1]
    lane = lax.broadcasted_iota(jnp.int32, x.shape, 1)
    first = (lane % (2 * half)) < half
    rot = jnp.where(first, pltpu.roll(x, w - half, 1), pltpu.roll(x, half, 1))
    return x * cos + rot * sin_signed


def _tok_tiling(n, t, tm):
    if t >= tm:
        assert t % tm == 0
        return 1, tm, t // tm
    assert tm % t == 0 and n % (tm // t) == 0
    return tm // t, t, 1


def _tok_spec(sb, tt, tps, width):
    return pl.BlockSpec((sb, tt, width), lambda i, *_: (i // tps, i % tps, 0))


def _seq_spec(sb, tps, rows, width):
    return pl.BlockSpec((sb, rows, width), lambda i, *_: (i // tps, 0, 0))


def _tab_spec(sb, tt, tps, width):
    return pl.BlockSpec((sb * tt, width), lambda i, *_: (i % tps, 0))


def _const_spec(shape):
    nd = len(shape)
    return pl.BlockSpec(shape, lambda *_: (0,) * nd)


def _mod_kernel(c_ref, w_ref, b_ref, o_ref):
    a = _silu(c_ref[...]).astype(BF16)
    o_ref[0] = _dot(a, w_ref[0]) + b_ref[0]


def _mod_call(c_all, w_mod16, b_mod):
    depth = w_mod16.shape[0]
    nc = c_all.shape[0]
    ncol = w_mod16.shape[2]
    tn = 1024
    return pl.pallas_call(
        _mod_kernel,
        grid=(depth, ncol // tn),
        in_specs=[pl.BlockSpec((nc, D_MODEL), lambda l, j: (0, 0)),
                  pl.BlockSpec((1, D_MODEL, tn), lambda l, j: (l, 0, j)),
                  pl.BlockSpec((1, 1, tn), lambda l, j: (l, 0, j))],
        out_specs=pl.BlockSpec((1, nc, tn), lambda l, j: (l, 0, j)),
        out_shape=jax.ShapeDtypeStruct((depth, nc, ncol), F32),
        compiler_params=_cparams("arbitrary", "arbitrary"),
        name="adaln_mod",
    )(c_all, w_mod16, b_mod.reshape(depth, 1, ncol))


def _ffn_kernel(x_ref, mod_ref, g_ref, wg_ref, wu_ref, wd_ref, *rest, sub, emit_h):
    if emit_h:
        o_ref, h_ref, hn_ref, acc_ref = rest
    else:
        o_ref, hn_ref, acc_ref = rest
    k = pl.program_id(1)
    sb, tt, d = x_ref.shape

    @pl.when(k == 0)
    def _():
        x = x_ref[...]
        shift = mod_ref[:, 3 * sub:3 * sub + 1, :]
        scale = mod_ref[:, 3 * sub + 1:3 * sub + 2, :]
        h = _rms(x, g_ref[2 * sub:2 * sub + 1, :]) * (1.0 + scale) + shift
        hn_ref[...] = h.reshape(sb * tt, d).astype(BF16)
        acc_ref[...] = jnp.zeros_like(acc_ref)

    hn = hn_ref[...]
    gte = _dot(hn, wg_ref[...])
    up = _dot(hn, wu_ref[...])
    act = (_silu(gte) * up).astype(BF16)
    acc_ref[...] += _dot(act, wd_ref[...])

    @pl.when(k == pl.num_programs(1) - 1)
    def _():
        y = acc_ref[...].reshape(sb, tt, d)
        gate = mod_ref[:, 3 * sub + 2:3 * sub + 3, :]
        out = x_ref[...] + FFN_RESID * (gate * _rms(y, g_ref[2 * sub + 1:2 * sub + 2, :]))
        o_ref[...] = out
        if emit_h:
            hm = _rms(out, g_ref[2:3, :]) * (1.0 + mod_ref[:, 4:5, :]) + mod_ref[:, 3:4, :]
            h_ref[...] = hm.astype(BF16)


def _ffn_call(x, mod, norm_g, wg16, wu16, wd16, layer, which, sub, emit_h, tm=512, ck=1408):
    n, t, d = x.shape
    sb, tt, tps = _tok_tiling(n, t, tm)
    grid = ((n // sb) * tps, D_FF // ck)
    out_shape = [jax.ShapeDtypeStruct((n, t, d), F32)]
    out_specs = [_tok_spec(sb, tt, tps, d)]
    if emit_h:
        out_shape.append(jax.ShapeDtypeStruct((n, t, d), BF16))
        out_specs.append(_tok_spec(sb, tt, tps, d))
    res = pl.pallas_call(
        functools.partial(_ffn_kernel, sub=sub, emit_h=emit_h),
        grid=grid,
        in_specs=[_tok_spec(sb, tt, tps, d),
                  _seq_spec(sb, tps, 9, d),
                  pl.BlockSpec((6, d), lambda i, k: (0, 0)),
                  pl.BlockSpec((None, None, d, ck), lambda i, k: (layer, which, 0, k)),
                  pl.BlockSpec((None, None, d, ck), lambda i, k: (layer, which, 0, k)),
                  pl.BlockSpec((None, None, ck, d), lambda i, k: (layer, which, k, 0))],
        out_specs=out_specs,
        out_shape=out_shape,
        scratch_shapes=[pltpu.VMEM((sb * tt, d), BF16), pltpu.VMEM((sb * tt, d), F32)],
        compiler_params=_cparams("parallel", "arbitrary"),
        name="ffn",
    )(x, mod, norm_g, wg16, wu16, wd16)
    return res if emit_h else res[0]


def _proj_mla_kernel(h_ref, wa_ref, gq_ref, gkv_ref, wuq_ref, wabs_ref, sel_ref,
                     cq_ref, sq_ref, ck_ref, sk_ref, qcat_ref, kcat_ref, ckv_ref, krope_ref):
    sb, tt, d = h_ref.shape
    tm = sb * tt
    z = _dot(h_ref[...].reshape(tm, d), wa_ref[...])
    cq = _rms(z[:, :MLA_Q_LORA], gq_ref[...])
    qh = _dot(cq.astype(BF16), wuq_ref[...])
    qn = (qh[:, :512] * MLA_SCALE).astype(BF16)
    qr = (_rope(qh[:, 512:], cq_ref[...], sq_ref[...], MLA_ROPE // 2) * MLA_SCALE).astype(BF16)
    qcat = _dot(qn, wabs_ref[...]) + _dot(qr, sel_ref[...])
    qcat_ref[...] = qcat.astype(BF16).reshape(sb, tt, qcat.shape[-1])
    ckv = _rms(z[:, 256:384], gkv_ref[...])
    kr = _rope(z[:, 384:512], ck_ref[...], sk_ref[...], MLA_ROPE // 2)
    ckv_ref[...] = ckv.reshape(sb, tt, MLA_KV_LORA)
    krope_ref[...] = kr[:, :MLA_ROPE].reshape(sb, tt, MLA_ROPE)
    kcat_ref[...] = jnp.concatenate([ckv, kr], axis=1).astype(BF16).reshape(sb, tt, 256)


def _proj_dsa_kernel(h_ref, wb_ref, c_ref, s_ref, ckw_ref, skw_ref,
                     qb_ref, kb_ref, vb_ref, qi_ref, kw_ref, ki_ref):
    sb, tt, d = h_ref.shape
    tm = sb * tt
    z = _dot(h_ref[...].reshape(tm, d), wb_ref[...])
    c1, s1 = c_ref[...], s_ref[...]
    c2, s2 = jnp.concatenate([c1, c1], axis=1), jnp.concatenate([s1, s1], axis=1)
    c4, s4 = jnp.concatenate([c2, c2], axis=1), jnp.concatenate([s2, s2], axis=1)
    half = DSA_HEAD_DIM // 2
    qb = _rope(z[:, :512], c4, s4, half) * DSA_SCALE
    qb_ref[...] = qb.astype(BF16).reshape(sb, tt, 512)
    kb_ref[...] = _rope(z[:, 512:640], c1, s1, half).reshape(sb, tt, 128)
    vb_ref[...] = z[:, 640:768].reshape(sb, tt, 128)
    qi_ref[...] = _rope(z[:, 768:1024], c2, s2, half).reshape(sb, tt, 256)
    kw = _rope(z[:, 1024:1152], ckw_ref[...], skw_ref[...], half)
    kw_ref[...] = kw.reshape(sb, tt, 128)
    ki_ref[...] = kw[:, :IDX_DIM].reshape(sb, tt, IDX_DIM)


def _proj_gla_kernel(h_ref, wc_ref, wga_ref, bga_ref, qc_ref, kc_ref, vc_ref, la_ref, gg_ref):
    sb, tt, d = h_ref.shape
    tm = sb * tt
    z = _dot(h_ref[...].reshape(tm, d), wc_ref[...])
    qc_ref[...] = (z[:, :256] * (GLA_DK ** -0.5)).reshape(sb, tt, 256)
    kc_ref[...] = z[:, 256:512].reshape(sb, tt, 256)
    vc_ref[...] = z[:, 512:1024].reshape(sb, tt, 512)
    gg_ref[...] = z[:, 1024:1536].reshape(sb, tt, 512)
    logit = _dot(z[:, 1536:1664].astype(BF16), wga_ref[...]) + bga_ref[...]
    la_ref[...] = (_log_sigmoid(logit) / GLA_TAU).reshape(sb, tt, 256)


def _proj_calls(h, lw, tabs, tm=512):
    n, t, d = h.shape
    sb, tt, tps = _tok_tiling(n, t, tm)
    grid = ((n // sb) * tps,)
    tok = lambda w: _tok_spec(sb, tt, tps, w)
    tab = lambda w: _tab_spec(sb, tt, tps, w)
    shp = lambda w, dt=F32: jax.ShapeDtypeStruct((n, t, w), dt)
    cp = _cparams("parallel")

    qcat, kcat, ckv, krope = pl.pallas_call(
        _proj_mla_kernel, grid=grid,
        in_specs=[tok(d), _const_spec((d, 512)), _const_spec((1, 256)), _const_spec((1, 128)),
                  _const_spec((256, 768)), _const_spec((512, 2048)), _const_spec((256, 2048)),
                  tab(256), tab(256), tab(128), tab(128)],
        out_specs=[tok(2048), tok(256), tok(128), tok(32)],
        out_shape=[shp(2048, BF16), shp(256, BF16), shp(128), shp(32)],
        compiler_params=cp, name="proj_mla",
    )(h, lw['wa'], lw['g_q'], lw['g_kv'], lw['w_uq'], lw['wabs'], lw['sel'],
      tabs['cos32q'], tabs['sin32q'], tabs['cos32k'], tabs['sin32k'])

    qb, kb, vb, qi, kw, ki = pl.pallas_call(
        _proj_dsa_kernel, grid=grid,
        in_specs=[tok(d), _const_spec((d, 1152)), tab(128), tab(128), tab(128), tab(128)],
        out_specs=[tok(512), tok(128), tok(128), tok(256), tok(128), tok(64)],
        out_shape=[shp(512, BF16), shp(128), shp(128), shp(256), shp(128), shp(64)],
        compiler_params=cp, name="proj_dsa",
    )(h, lw['wb'], tabs['cos64'], tabs['sin64'], tabs['cos64kw'], tabs['sin64kw'])

    qc, kc, vc, la, gg = pl.pallas_call(
        _proj_gla_kernel, grid=grid,
        in_specs=[tok(d), _const_spec((d, 1664)), _const_spec((128, 256)), _const_spec((1, 256))],
        out_specs=[tok(256), tok(256), tok(512), tok(256), tok(512)],
        out_shape=[shp(256), shp(256), shp(512), shp(256), shp(512)],
        compiler_params=cp, name="proj_gla",
    )(h, lw['wc'], lw['wga'], lw['bga'])
    return dict(qcat=qcat, kcat=kcat, ckv=ckv, krope=krope, qb=qb, kb=kb, vb=vb, qi=qi, kw=kw, ki=ki,
                qc=qc, kc=kc, vc=vc, la=la, gg=gg)


def _mla_p_kernel(q_ref, k_ref, wuv_ref, o_ref, m_ref, l_ref, acc_ref, *, tq, tk):
    qi = pl.program_id(1)
    nh = MLA_HEADS
    q = jnp.concatenate([q_ref[0, :, 256 * h:256 * (h + 1)] for h in range(nh)], axis=0)
    m_ref[...] = jnp.full_like(m_ref, -jnp.inf)
    l_ref[...] = jnp.zeros_like(l_ref)
    acc_ref[...] = jnp.zeros_like(acc_ref)

    def step(j, mask):
        start = pl.multiple_of(j * tk, tk)
        k = k_ref[0, pl.ds(start, tk), :]
        s = _dot_nt(q, k)
        if mask is not None:
            s = jnp.where(mask, s, -jnp.inf)
        m_prev = m_ref[...]
        m_new = jnp.maximum(m_prev, jnp.max(s, axis=1, keepdims=True))
        alpha = jnp.exp(m_prev - m_new)
        p = jnp.exp(s - m_new)
        l_ref[...] = alpha * l_ref[...] + jnp.sum(p, axis=1, keepdims=True)
        acc_ref[...] = alpha * acc_ref[...] + _dot(p.astype(BF16), k[:, :MLA_KV_LORA])
        m_ref[...] = m_new

    nfull = qi * (tq // tk)

    def body(j, c):
        step(j, None)
        return c

    lax.fori_loop(0, nfull, body, 0)
    row = lax.broadcasted_iota(jnp.int32, (tq, tk), 0)
    col = lax.broadcasted_iota(jnp.int32, (tq, tk), 1)
    for dj in range(tq // tk):
        mk = row >= col + dj * tk
        step(nfull + dj, jnp.concatenate([mk] * nh, axis=0))

    o_lat = (acc_ref[...] / l_ref[...]).astype(BF16)
    out = _dot(o_lat[:tq], wuv_ref[0])
    for h in range(1, nh):
        out = out + _dot(o_lat[h * tq:(h + 1) * tq], wuv_ref[h])
    o_ref[0] = out.astype(BF16)


def _mla_prompt_call(qcat, kcat, wuv_pad, tq=256, tk=256):
    n, t, _ = qcat.shape
    return pl.pallas_call(
        functools.partial(_mla_p_kernel, tq=tq, tk=tk),
        grid=(n, t // tq),
        in_specs=[pl.BlockSpec((1, tq, 2048), lambda b, i: (b, i, 0)),
                  pl.BlockSpec((1, t, 256), lambda b, i: (b, 0, 0)),
                  _const_spec((MLA_HEADS, 128, 512))],
        out_specs=pl.BlockSpec((1, tq, 512), lambda b, i: (b, i, 0)),
        out_shape=jax.ShapeDtypeStruct((n, t, 512), BF16),
        scratch_shapes=[pltpu.VMEM((MLA_HEADS * tq, 1), F32), pltpu.VMEM((MLA_HEADS * tq, 1), F32),
                        pltpu.VMEM((MLA_HEADS * tq, MLA_KV_LORA), F32)],
        compiler_params=_cparams("parallel", "arbitrary"),
        name="mla_prompt",
    )(qcat, kcat, wuv_pad)


def _sort_key(score):
    bits = lax.bitcast_convert_type(score + 0.0, jnp.int32)
    return jnp.where(bits < 0, bits ^ jnp.int32(0x7FFFFFFF), bits)


def _count(mask):
    return jnp.sum(jnp.where(mask, 1.0, 0.0), axis=1, keepdims=True)


def _topk_mask(key, kk, idx_bits):
    rows, width = key.shape
    kkf = jnp.float32(kk)
    thr = jnp.where(_count(key >= 0) >= kkf, jnp.int32(0), jnp.int32(INT_MIN))

    def vbody(i, thr):
        cand = thr | lax.shift_left(jnp.int32(1), jnp.int32(30) - i)
        return jnp.where(_count(key >= cand) >= kkf, cand, thr)

    thr = lax.fori_loop(0, 31, vbody, thr)
    gt = key > thr
    eq = key == thr
    need = kkf - _count(gt)
    idx = lax.broadcasted_iota(jnp.int32, (rows, width), 1)

    def pbody(i, p):
        cand = p | lax.shift_left(jnp.int32(1), jnp.int32(idx_bits - 1) - i)
        return jnp.where(_count(eq & (idx < cand)) <= need, cand, p)

    p = lax.fori_loop(0, idx_bits, pbody, jnp.zeros((rows, 1), jnp.int32))
    return gt | (eq & (idx < p))


def _split3_q(qf, ph_ref, plo_ref):
    q_hi = qf.astype(BF16)
    q_lo = (qf - q_hi.astype(F32)).astype(BF16)
    return jnp.concatenate(
        [_dot(q_hi, ph_ref[h]) + _dot(q_lo, plo_ref[h]) for h in range(IDX_HEADS)], axis=0).astype(BF16)


def _split3_k(kk, pkh_ref, pkl_ref):
    w = kk.shape[1]
    k_hi = kk.astype(BF16)
    k_lo = (kk - k_hi.astype(F32)).astype(BF16)
    return (_dot(k_hi, pkh_ref[:w, :]) + _dot(k_lo, pkl_ref[:w, :])).astype(BF16)


def _idx_scores(q3, k3, wq, tq):
    dots = _dot_nt(q3, k3)
    score = jnp.zeros((tq, k3.shape[0]), F32)
    for h in range(IDX_HEADS):
        wh = wq[:, IDX_DIM + h:IDX_DIM + h + 1] * IDX_SCALE
        score = score + wh * jnp.maximum(dots[h * tq:(h + 1) * tq], 0.0)
    return score


def _dsa_p_kernel(qi_ref, kwq_ref, kwk_ref, qb_ref, kb_ref, vb_ref, ph_ref, plo_ref, pkh_ref, pkl_ref,
                  pq_ref, po_ref, o_ref, k3_ref, kb16_ref, vb16_ref, *, tq, topk):
    it = pl.program_id(1)
    s_len = kwk_ref.shape[1]

    @pl.when(it == 0)
    def _():
        k3_ref[...] = _split3_k(kwk_ref[0], pkh_ref, pkl_ref)
        kb16_ref[...] = kb_ref[0].astype(BF16)
        vb16_ref[...] = vb_ref[0].astype(BF16)

    score = _idx_scores(_split3_q(qi_ref[0], ph_ref, plo_ref), k3_ref[...], kwq_ref[0], tq)
    qpos = it * tq + lax.broadcasted_iota(jnp.int32, (tq, s_len), 0)
    kpos = lax.broadcasted_iota(jnp.int32, (tq, s_len), 1)
    causal = kpos <= qpos
    key = _sort_key(jnp.where(causal, score, -jnp.inf))
    mask = _topk_mask(key, topk, int(s_len).bit_length()) & causal

    qb = qb_ref[0]
    kb = kb16_ref[...]
    vb = vb16_ref[...]
    grp = DSA_HEADS // DSA_KV_HEADS
    mask4 = jnp.concatenate([mask] * grp, axis=0)
    out = jnp.zeros((tq, 512), F32)
    for j in range(DSA_KV_HEADS):
        qp = _dot(qb, pq_ref[j]).astype(BF16)
        qrows = jnp.concatenate([qp[:, 128 * g:128 * (g + 1)] for g in range(grp)], axis=0)
        s = jnp.where(mask4, _dot_nt(qrows, kb), -jnp.inf)
        m = jnp.max(s, axis=1, keepdims=True)
        p = jnp.exp(s - m)
        l = jnp.sum(p, axis=1, keepdims=True)
        o = (_dot(p.astype(BF16), vb) / l).astype(BF16)
        for g in range(grp):
            out = out + _dot(o[g * tq:(g + 1) * tq], po_ref[j * grp + g])
    o_ref[0] = out.astype(BF16)


def _dsa_prompt_call(pr, consts, tq=128):
    n, t, _ = pr['qi'].shape
    topk = min(IDX_TOPK_MAX, t // 4)
    qspec = lambda w: pl.BlockSpec((1, tq, w), lambda b, i: (b, i, 0))
    sspec = lambda w: pl.BlockSpec((1, t, w), lambda b, i: (b, 0, 0))
    return pl.pallas_call(
        functools.partial(_dsa_p_kernel, tq=tq, topk=topk),
        grid=(n, t // tq),
        in_specs=[qspec(256), qspec(128), sspec(128), qspec(512), sspec(128), sspec(128),
                  _const_spec((4, 256, 256)), _const_spec((4, 256, 256)), _const_spec((128, 256)),
                  _const_spec((128, 256)), _const_spec((2, 512, 512)), _const_spec((8, 128, 512))],
        out_specs=qspec(512),
        out_shape=jax.ShapeDtypeStruct((n, t, 512), BF16),
        scratch_shapes=[pltpu.VMEM((t, 256), BF16), pltpu.VMEM((t, 128), BF16), pltpu.VMEM((t, 128), BF16)],
        compiler_params=_cparams("parallel", "arbitrary"),
        name="dsa_prompt",
    )(pr['qi'], pr['kw'], pr['kw'], pr['qb'], pr['kb'], pr['vb'],
      consts['ph'], consts['pl'], consts['pkh'], consts['pkl'], consts['pq'], consts['po'])


def _gla_kernel(q_ref, k_ref, v_ref, la_ref, s0_ref, ones_ref, o_ref, sfin_ref, st_ref, *, chunk, tt):
    ti = pl.program_id(1)
    nh, dk, dv = GLA_HEADS, GLA_DK, GLA_DV
    hd = nh * dk

    @pl.when(ti == 0)
    def _():
        st_ref[...] = s0_ref[0].astype(F32).reshape(hd, dv)

    q = q_ref[0]
    k = k_ref[0]
    v = v_ref[0]
    la = la_ref[0]
    rc = lax.broadcasted_iota(jnp.int32, (tt, hd), 0) % chunk
    rcv = lax.broadcasted_iota(jnp.int32, (tt, nh * dv), 0) % chunk

    b = la
    sft = 1
    while sft < chunk:
        b = b + jnp.where(rc >= sft, pltpu.roll(b, sft, 0), 0.0)
        sft *= 2
    bl = jnp.where(rc == chunk - 1, b, 0.0)
    sft = 1
    while sft < chunk:
        bl = bl + jnp.where(rc < chunk - sft, pltpu.roll(bl, tt - sft, 0), 0.0)
        sft *= 2

    ones = ones_ref[...]
    o_intra = jnp.zeros((tt, nh * dv), F32)
    for lag in range(chunk):
        if lag == 0:
            pr = q * k
            vs = v
        else:
            valid = rc >= lag
            ks = pltpu.roll(k, lag, 0)
            bs = pltpu.roll(b, lag, 0)
            pr = jnp.where(valid, q * ks * jnp.exp(jnp.minimum(b - bs, 0.0)), 0.0)
            vs = jnp.where(rcv >= lag, pltpu.roll(v, lag, 0), 0.0)
        p_hi = pr.astype(BF16)
        p_lo = (pr - p_hi.astype(F32)).astype(BF16)
        att = _dot(p_hi, ones) + _dot(p_lo, ones)
        o_intra = o_intra + att * vs

    qe = q * jnp.exp(b)
    kd = k * jnp.exp(bl - b)
    pad = LANES - tt
    if pad:
        kd = jnp.concatenate([kd, jnp.zeros((pad, hd), F32)], axis=0)
        vp = jnp.concatenate([v, jnp.zeros((pad, nh * dv), F32)], axis=0)
        blp = jnp.concatenate([bl, jnp.zeros((pad, hd), F32)], axis=0)
    else:
        vp, blp = v, bl
    kdt = kd.T.astype(BF16)
    dect = jnp.exp(blp.T)
    v16 = vp.astype(BF16)
    lane = lax.broadcasted_iota(jnp.int32, (hd, LANES), 1)
    rowh = lax.broadcasted_iota(jnp.int32, (nh * chunk, hd), 0) // chunk
    colh = lax.broadcasted_iota(jnp.int32, (nh * chunk, hd), 1) // dk
    headmask = rowh == colh
    st = st_ref[...]
    outs = []
    for c in range(tt // chunk):
        qe_c = qe[c * chunk:(c + 1) * chunk]
        x = jnp.where(headmask, jnp.concatenate([qe_c] * nh, axis=0), 0.0).astype(BF16)
        r = _dot(x, st.astype(BF16))
        outs.append(jnp.concatenate([r[h * chunk:(h + 1) * chunk] for h in range(nh)], axis=1))
        inchunk = (lane >= c * chunk) & (lane < (c + 1) * chunk)
        u = _dot(jnp.where(inchunk, kdt, jnp.zeros_like(kdt)), v16)
        upd = jnp.concatenate([u[h * dk:(h + 1) * dk, h * dv:(h + 1) * dv] for h in range(nh)], axis=0)
        last = (c + 1) * chunk - 1
        st = dect[:, last:last + 1] * st + upd
    st_ref[...] = st
    o_inter = outs[0] if len(outs) == 1 else jnp.concatenate(outs, axis=0)
    o_ref[0] = o_inter + o_intra

    @pl.when(ti == pl.num_programs(1) - 1)
    def _():
        sfin_ref[0] = st.reshape(nh, dk, dv)


def _gla_call(qc, kc, vc, la, s0, ones_blk):
    n, t, _ = qc.shape
    chunk = int(np.gcd(t, GLA_CHUNK))
    tt = min(t, LANES)
    assert t % tt == 0 and tt % chunk == 0
    tspec = lambda w: pl.BlockSpec((1, tt, w), lambda b, i: (b, i, 0))
    sspec = pl.BlockSpec((1, GLA_HEADS, GLA_DK, GLA_DV), lambda b, i: (b, 0, 0, 0))
    return pl.pallas_call(
        functools.partial(_gla_kernel, chunk=chunk, tt=tt),
        grid=(n, t // tt),
        in_specs=[tspec(256), tspec(256), tspec(512), tspec(256), sspec, _const_spec((256, 512))],
        out_specs=[tspec(512), sspec],
        out_shape=[jax.ShapeDtypeStruct((n, t, 512), F32),
                   jax.ShapeDtypeStruct((n, GLA_HEADS, GLA_DK, GLA_DV), F32)],
        scratch_shapes=[pltpu.VMEM((GLA_HEADS * GLA_DK, GLA_DV), F32)],
        compiler_params=_cparams("parallel", "arbitrary"),
        name="gla",
    )(qc, kc, vc, la, s0, ones_blk)


def _mix_kernel(h_ref, x_ref, mod_ref, g_ref, oa_ref, ob_ref, oc_ref, gg_ref, ggla_ref,
                wm_ref, wbr_ref, wo_ref, o_ref):
    sb, tt, d = x_ref.shape
    tm = sb * tt
    h = h_ref[...].reshape(tm, d)
    gates = _sigmoid(_dot(h, wm_ref[...]))
    oc = oc_ref[...].reshape(tm, BRANCH_W)
    gg = gg_ref[...].reshape(tm, BRANCH_W)
    ggla = ggla_ref[...]
    parts = []
    for hh in range(GLA_HEADS):
        seg = oc[:, GLA_DV * hh:GLA_DV * (hh + 1)]
        parts.append(_rms(seg, ggla))
    ocn = (jnp.concatenate(parts, axis=1) * _silu(gg)).astype(BF16)
    ya = _dot(oa_ref[...].reshape(tm, BRANCH_W), wbr_ref[0])
    yb = _dot(ob_ref[...].reshape(tm, BRANCH_W), wbr_ref[1])
    yc = _dot(ocn, wbr_ref[2])
    y = gates[:, :d] * ya + gates[:, d:2 * d] * yb + gates[:, 2 * d:] * yc
    out = _dot(y.astype(BF16), wo_ref[...]).reshape(sb, tt, d)
    o_ref[...] = x_ref[...] + mod_ref[:, 5:6, :] * _rms(out, g_ref[3:4, :])


def _mix_call(h, x, mod, norm_g, oa, ob, oc, gg, lw, tm=256):
    n, t, d = x.shape
    sb, tt, tps = _tok_tiling(n, t, tm)
    tok = lambda w: _tok_spec(sb, tt, tps, w)
    return pl.pallas_call(
        _mix_kernel,
        grid=((n // sb) * tps,),
        in_specs=[tok(d), tok(d), _seq_spec(sb, tps, 9, d), _const_spec((6, d)),
                  tok(512), tok(512), tok(512), tok(512), _const_spec((1, 128)),
                  _const_spec((d, 3 * d)), _const_spec((3, 512, d)), _const_spec((d, d))],
        out_specs=tok(d),
        out_shape=jax.ShapeDtypeStruct((n, t, d), F32),
        compiler_params=_cparams("parallel"),
        name="mix_out",
    )(h, x, mod, norm_g, oa, ob, oc, gg, lw['g_gla'], lw['wm'], lw['wbr'], lw['wo'])


def _smp1_kernel(pt_ref, q_ref, qi_ref, kwq_ref, kcn_ref, kwn_ref, ph_ref, plo_ref, pkh_ref, pkl_ref, wuv_ref,
                 *rest, pps, t_new):
    lat = rest[:pps]
    kro = rest[pps:2 * pps]
    ikp = rest[2 * pps:3 * pps]
    o_ref, sc_ref, m_ref, l_ref, acc_ref = rest[3 * pps:]
    j = pl.program_id(1)
    nj = pl.num_programs(1)
    nh = MLA_HEADS
    tq = t_new

    @pl.when(j == 0)
    def _():
        m_ref[...] = jnp.full_like(m_ref, -jnp.inf)
        l_ref[...] = jnp.zeros_like(l_ref)
        acc_ref[...] = jnp.zeros_like(acc_ref)

    qall = q_ref[0].astype(F32)
    qa = jnp.concatenate([qall[:, 256 * h:256 * h + 128] for h in range(nh)], axis=0).astype(BF16)
    qr = jnp.concatenate([qall[:, 256 * h + 128:256 * h + 256] for h in range(nh)], axis=0).astype(BF16)
    q3 = _split3_q(qi_ref[0], ph_ref, plo_ref)
    wq = kwq_ref[0]

    def idx_scores(kslab):
        return _idx_scores(q3, _split3_k(kslab, pkh_ref, pkl_ref), wq, tq)

    def flash(s, vals):
        m_prev = m_ref[...]
        m_new = jnp.maximum(m_prev, jnp.max(s, axis=1, keepdims=True))
        alpha = jnp.exp(m_prev - m_new)
        p = jnp.exp(s - m_new)
        l_ref[...] = alpha * l_ref[...] + jnp.sum(p, axis=1, keepdims=True)
        acc_ref[...] = alpha * acc_ref[...] + _dot(p.astype(BF16), vals)
        m_ref[...] = m_new

    @pl.when(j < nj - 1)
    def _():
        c = jnp.concatenate([lat[i][...] for i in range(pps)], axis=0).astype(BF16)
        r = jnp.concatenate([kro[i][...] for i in range(pps)], axis=0).astype(BF16)
        flash(_dot_nt(qa, c) + _dot_nt(qr[:, :MLA_ROPE], r), c)
        sc_ref[0] = idx_scores(jnp.concatenate([ikp[i][...] for i in range(pps)], axis=0))

    @pl.when(j == nj - 1)
    def _():
        kc = jnp.concatenate([kcn_ref[0].astype(F32), jnp.zeros((LANES - tq, 256), F32)], axis=0).astype(BF16)
        s = _dot_nt(jnp.concatenate([qa, qr], axis=1), kc)
        row = lax.broadcasted_iota(jnp.int32, (nh * tq, LANES), 0) % tq
        col = lax.broadcasted_iota(jnp.int32, (nh * tq, LANES), 1)
        flash(jnp.where(col <= row, s, -jnp.inf), kc[:, :MLA_KV_LORA])
        o_lat = acc_ref[...] / l_ref[...]
        out = jnp.zeros((tq, 512), F32)
        for h in range(nh):
            out = out + _dot(o_lat[h * tq:(h + 1) * tq].astype(BF16), wuv_ref[h])
        o_ref[0] = out.astype(BF16)
        kslab = jnp.concatenate([kwn_ref[0], jnp.zeros((LANES - tq, LANES), F32)], axis=0)
        sc = idx_scores(kslab)
        r8 = lax.broadcasted_iota(jnp.int32, (tq, LANES), 0)
        c8 = lax.broadcasted_iota(jnp.int32, (tq, LANES), 1)
        tail = jnp.where(c8 <= r8, sc, -jnp.inf)
        fill = jnp.full((tq, pps * PAGE_SIZE - LANES), -jnp.inf, F32)
        sc_ref[0] = jnp.concatenate([tail, fill], axis=1)


def _smp1_call(layer, pr, page_table, lat_pool, kr_pool, ik_pool, consts, wuv_pad, pps=8):
    n, t, _ = pr['qcat'].shape
    n_pages = page_table.shape[1]
    assert n_pages % pps == 0
    nj = n_pages // pps + 1
    width = nj * pps * PAGE_SIZE

    def page_spec(i, w):
        return pl.BlockSpec((None, None, PAGE_SIZE, w),
                            lambda b, j, pt: (layer, pt[b, jnp.minimum(j * pps + i, n_pages - 1)], 0, 0))

    nspec = lambda w: pl.BlockSpec((1, t, w), lambda b, j, pt: (b, 0, 0))
    grid_spec = pltpu.PrefetchScalarGridSpec(
        num_scalar_prefetch=1,
        grid=(n, nj),
        in_specs=[nspec(2048), nspec(256), nspec(128), nspec(256), nspec(128),
                  _const_spec((4, 256, 256)), _const_spec((4, 256, 256)), _const_spec((128, 256)),
                  _const_spec((128, 256)), _const_spec((MLA_HEADS, 128, 512))]
                 + [page_spec(i, MLA_KV_LORA) for i in range(pps)]
                 + [page_spec(i, MLA_ROPE) for i in range(pps)]
                 + [page_spec(i, IDX_DIM) for i in range(pps)],
        out_specs=[pl.BlockSpec((1, t, 512), lambda b, j, pt: (b, 0, 0)),
                   pl.BlockSpec((1, t, pps * PAGE_SIZE), lambda b, j, pt: (b, 0, j))],
        scratch_shapes=[pltpu.VMEM((MLA_HEADS * t, 1), F32), pltpu.VMEM((MLA_HEADS * t, 1), F32),
                        pltpu.VMEM((MLA_HEADS * t, MLA_KV_LORA), F32)],
    )
    return pl.pallas_call(
        functools.partial(_smp1_kernel, pps=pps, t_new=t),
        grid_spec=grid_spec,
        out_shape=[jax.ShapeDtypeStruct((n, t, 512), BF16), jax.ShapeDtypeStruct((n, t, width), F32)],
        compiler_params=_cparams("parallel", "arbitrary"),
        name="mla_idx_sample",
    )(page_table, pr['qcat'], pr['qi'], pr['kw'], pr['kcat'], pr['kw'],
      consts['ph'], consts['pl'], consts['pkh'], consts['pkl'], wuv_pad,
      *([lat_pool] * pps), *([kr_pool] * pps), *([ik_pool] * pps))


def _smp2_kernel(pt_ref, sc_ref, qb_ref, kbn_ref, vbn_ref, pq_ref, po_ref, *rest, pps, t_new, topk):
    kp = rest[:pps]
    vp = rest[pps:2 * pps]
    o_ref, mask_ref, q_scr, m_ref, l_ref, acc_ref = rest[2 * pps:]
    j = pl.program_id(1)
    nj = pl.num_programs(1)
    tq = t_new
    grp = DSA_HEADS // DSA_KV_HEADS
    nrow = DSA_KV_HEADS * grp * tq
    width = sc_ref.shape[2]
    cw = pps * PAGE_SIZE

    @pl.when(j == 0)
    def _():
        key = _sort_key(sc_ref[0])
        sel = _topk_mask(key, topk, int(width).bit_length())
        mask_ref[...] = jnp.where(sel & (sc_ref[0] > -jnp.inf), 1.0, 0.0)
        m_ref[...] = jnp.full_like(m_ref, -jnp.inf)
        l_ref[...] = jnp.zeros_like(l_ref)
        acc_ref[...] = jnp.zeros_like(acc_ref)
        qb = qb_ref[0]
        rows = []
        for jj in range(DSA_KV_HEADS):
            qp = _dot(qb, pq_ref[jj])
            rows += [qp[:, 128 * g:128 * (g + 1)] for g in range(grp)]
        q_scr[...] = jnp.concatenate(rows, axis=0).astype(BF16)

    def flash(s, vals):
        m_prev = m_ref[...]
        m_new = jnp.maximum(m_prev, jnp.max(s, axis=1, keepdims=True))
        m_safe = jnp.where(m_new == -jnp.inf, 0.0, m_new)
        alpha = jnp.exp(m_prev - m_safe)
        p = jnp.exp(s - m_safe)
        l_ref[...] = alpha * l_ref[...] + jnp.sum(p, axis=1, keepdims=True)
        acc_ref[...] = alpha * acc_ref[...] + _dot(p.astype(BF16), vals)
        m_ref[...] = m_new

    q = q_scr[...]
    start = pl.multiple_of(j * cw, cw)
    msk = mask_ref[:, pl.ds(start, cw)]
    msk = jnp.concatenate([msk] * (DSA_KV_HEADS * grp), axis=0) > 0.5

    @pl.when(j < nj - 1)
    def _():
        k = jnp.concatenate([kp[i][...] for i in range(pps)], axis=0).astype(BF16)
        v = jnp.concatenate([vp[i][...] for i in range(pps)], axis=0).astype(BF16)
        flash(jnp.where(msk, _dot_nt(q, k), -jnp.inf), v)

    @pl.when(j == nj - 1)
    def _():
        zr = jnp.zeros((LANES - tq, LANES), F32)
        k = jnp.concatenate([kbn_ref[0], zr], axis=0).astype(BF16)
        v = jnp.concatenate([vbn_ref[0], zr], axis=0).astype(BF16)
        flash(jnp.where(msk[:, :LANES], _dot_nt(q, k), -jnp.inf), v)
        o = acc_ref[...] / l_ref[...]
        out = jnp.zeros((tq, 512), F32)
        for jj in range(DSA_KV_HEADS):
            for g in range(grp):
                r0 = (jj * grp + g) * tq
                out = out + _dot(o[r0:r0 + tq].astype(BF16), po_ref[jj * grp + g])
        o_ref[0] = out.astype(BF16)


def _smp2_call(layer, pr, scores, page_table, k_pool, v_pool, consts, pps=8):
    n, t, _ = pr['qb'].shape
    n_pages = page_table.shape[1]
    nj = n_pages // pps + 1
    width = scores.shape[2]
    past = n_pages * PAGE_SIZE
    topk = min(IDX_TOPK_MAX, (past + t) // 4)
    nrow = DSA_HEADS * t

    def page_spec(i):
        return pl.BlockSpec((None, None, PAGE_SIZE, 128),
                            lambda b, j, pt: (layer, pt[b, jnp.minimum(j * pps + i, n_pages - 1)], 0, 0))

    nspec = lambda w: pl.BlockSpec((1, t, w), lambda b, j, pt: (b, 0, 0))
    grid_spec = pltpu.PrefetchScalarGridSpec(
        num_scalar_prefetch=1,
        grid=(n, nj),
        in_specs=[nspec(width), nspec(512), nspec(128), nspec(128),
                  _const_spec((2, 512, 512)), _const_spec((8, 128, 512))]
                 + [page_spec(i) for i in range(pps)] + [page_spec(i) for i in range(pps)],
        out_specs=pl.BlockSpec((1, t, 512), lambda b, j, pt: (b, 0, 0)),
        scratch_shapes=[pltpu.VMEM((t, width), F32), pltpu.VMEM((nrow, 128), BF16),
                        pltpu.VMEM((nrow, 1), F32), pltpu.VMEM((nrow, 1), F32), pltpu.VMEM((nrow, 128), F32)],
    )
    return pl.pallas_call(
        functools.partial(_smp2_kernel, pps=pps, t_new=t, topk=topk),
        grid_spec=grid_spec,
        out_shape=jax.ShapeDtypeStruct((n, t, 512), BF16),
        compiler_params=_cparams("parallel", "arbitrary"),
        name="dsa_sample",
    )(page_table, scores, pr['qb'], pr['kb'], pr['vb'], consts['pq'], consts['po'],
      *([k_pool] * pps), *([v_pool] * pps))


def _placement_consts():
    ph = np.zeros((4, 256, 256), np.float32)
    plo = np.zeros((4, 256, 256), np.float32)
    for h in range(4):
        for dd in range(64):
            ph[h, 64 * h + dd, dd] = 1.0
            ph[h, 64 * h + dd, 64 + dd] = 1.0
            plo[h, 64 * h + dd, 128 + dd] = 1.0
    pkh = np.zeros((128, 256), np.float32)
    pkl = np.zeros((128, 256), np.float32)
    for dd in range(64):
        pkh[dd, dd] = 1.0
        pkh[dd, 128 + dd] = 1.0
        pkl[dd, 64 + dd] = 1.0
    pq = np.zeros((2, 512, 512), np.float32)
    po = np.zeros((8, 128, 512), np.float32)
    for j in range(2):
        for g in range(4):
            head = 4 * j + g
            for e in range(64):
                pq[j, 64 * head + e, 128 * g + 64 * j + e] = 1.0
                po[head, 64 * j + e, 64 * head + e] = 1.0
    ones = np.zeros((256, 512), np.float32)
    for h in range(4):
        ones[64 * h:64 * (h + 1), 128 * h:128 * (h + 1)] = 1.0
    mk = lambda a: jnp.asarray(a, BF16)
    return dict(ph=mk(ph), pl=mk(plo), pkh=mk(pkh), pkl=mk(pkl), pq=mk(pq), po=mk(po), ones=mk(ones))


def _rope_tables(pos, rep):
    def base(half):
        inv = ROPE_THETA ** (-jnp.arange(half, dtype=F32) / half)
        ang = pos.astype(F32)[:, None] * inv[None, :]
        c, s = jnp.cos(ang), jnp.sin(ang)
        return jnp.concatenate([c, c], axis=1), jnp.concatenate([-s, s], axis=1)

    c32, s32 = base(16)
    c64, s64 = base(32)
    tpos = pos.shape[0]
    one = lambda w: jnp.ones((tpos, w), F32)
    zero = lambda w: jnp.zeros((tpos, w), F32)
    tabs = dict(
        cos32q=jnp.tile(c32, (1, 8)), sin32q=jnp.tile(s32, (1, 8)),
        cos32k=jnp.concatenate([c32, one(96)], axis=1), sin32k=jnp.concatenate([s32, zero(96)], axis=1),
        cos64=jnp.tile(c64, (1, 2)), sin64=jnp.tile(s64, (1, 2)),
        cos64kw=jnp.concatenate([c64, one(64)], axis=1), sin64kw=jnp.concatenate([s64, zero(64)], axis=1),
    )
    if rep > 1:
        tabs = {k: jnp.tile(v, (rep, 1)) for k, v in tabs.items()}
    return tabs


def _prep_layer(w_in, g_q, g_kv, w_uq, w_ukv, w_gla_a, b_gla_a, g_gla, w_branch, w_o):
    col = lambda nm: w_in[:, _IN_OFF[nm][0]:_IN_OFF[nm][1]]
    zc = lambda w: jnp.zeros((D_MODEL, w), w_in.dtype)
    wa = jnp.concatenate([col('mla_cq'), col('mla_ckv'), col('mla_krope'), zc(96)], axis=1)
    wb = jnp.concatenate([col('dsa_q'), col('dsa_k'), col('dsa_v'), col('idx_q'), col('idx_k'), col('idx_w'),
                          zc(60)], axis=1)
    wc = jnp.concatenate([col('gla_q'), col('gla_k'), col('gla_v'), col('gla_g'), col('gla_a'), zc(112)], axis=1)
    wm = col('merge')
    uq = w_uq.reshape(MLA_Q_LORA, MLA_HEADS, MLA_NOPE + MLA_ROPE)
    w_uq_r = jnp.concatenate([uq[:, :, :MLA_NOPE].reshape(MLA_Q_LORA, -1),
                              uq[:, :, MLA_NOPE:].reshape(MLA_Q_LORA, -1)], axis=1)
    w_uk = w_ukv[:, :, :MLA_NOPE]
    w_uv = w_ukv[:, :, MLA_NOPE:]
    wabs = jnp.zeros((MLA_HEADS * MLA_NOPE, MLA_HEADS * 256), w_in.dtype)
    wuv_pad = jnp.zeros((MLA_HEADS, MLA_KV_LORA, 512), w_in.dtype)
    sel = np.zeros((MLA_HEADS * MLA_ROPE, MLA_HEADS * 256), np.float32)
    for h in range(MLA_HEADS):
        wabs = wabs.at[MLA_NOPE * h:MLA_NOPE * (h + 1), 256 * h:256 * h + MLA_KV_LORA].set(w_uk[:, h, :].T)
        wuv_pad = wuv_pad.at[h, :, MLA_V * h:MLA_V * (h + 1)].set(w_uv[:, h, :])
        for dd in range(MLA_ROPE):
            sel[MLA_ROPE * h + dd, 256 * h + MLA_KV_LORA + dd] = 1.0
    wga = jnp.concatenate([w_gla_a, jnp.zeros((LANES - GLA_GATE_RANK, w_gla_a.shape[1]), w_gla_a.dtype)], axis=0)
    b16 = lambda a: a.astype(BF16)
    return dict(wa=b16(wa), wb=b16(wb), wc=b16(wc), wm=b16(wm), w_uq=b16(w_uq_r), wabs=b16(wabs),
                sel=jnp.asarray(sel, BF16), wuv_pad=b16(wuv_pad), wga=b16(wga),
                bga=b_gla_a.reshape(1, -1), g_q=g_q.reshape(1, -1), g_kv=g_kv.reshape(1, -1),
                g_gla=g_gla.reshape(1, -1), wbr=b16(w_branch), wo=b16(w_o))


TOKEN_TILE = 512
MIX_TILE = 256


def _trunk_layer(x, mod, ng, ffw, layer, lw, tabs, mixer_fn):
    n, t, _ = x.shape
    x, h = _ffn_call(x, mod, ng, *ffw, layer, 0, 0, True, tm=TOKEN_TILE)
    pr = _proj_calls(h, lw, tabs, tm=TOKEN_TILE)
    oa, ob, oc, st = mixer_fn(pr)
    x = _mix_call(h, x, mod, ng, oa, ob, oc, pr['gg'], lw, tm=MIX_TILE)
    x = _ffn_call(x, mod, ng, *ffw, layer, 1, 2, False, tm=TOKEN_TILE)
    kv_shape = (n, t, DSA_KV_HEADS, DSA_HEAD_DIM)
    return x, (pr['ckv'], pr['krope'], pr['kb'].reshape(kv_shape), pr['vb'].reshape(kv_shape), pr['ki'], st)


def kernel(x_prompt, x_sample, cache_mla_latent, cache_mla_krope, cache_dsa_k, cache_dsa_v, cache_dsa_idx_k,
           state_gla, page_table, c_prompt, c_sample, w_mod, b_mod, norm_g, w_ff_gate, w_ff_up, w_ff_down,
           w_in, g_q, g_kv, w_uq, w_ukv, w_gla_a, b_gla_a, g_gla, w_branch, w_o):
    depth = w_in.shape[0]
    nb, seq, _ = x_prompt.shape
    ndec, tdec, _ = x_sample.shape
    n_phys = cache_dsa_k.shape[1]
    past = page_table.shape[1] * PAGE_SIZE

    consts = _placement_consts()
    mod_all = _mod_call(jnp.concatenate([c_prompt, c_sample], axis=0), w_mod.astype(BF16), b_mod)
    ffw = (w_ff_gate.astype(BF16), w_ff_up.astype(BF16), w_ff_down.astype(BF16))
    tabs_p = _rope_tables(jnp.arange(seq), 1)
    tabs_s = _rope_tables(past + jnp.arange(tdec), max(1, TOKEN_TILE // tdec))
    k_pool = cache_dsa_k.reshape(depth, n_phys, PAGE_SIZE, DSA_KV_HEADS * DSA_HEAD_DIM)
    v_pool = cache_dsa_v.reshape(depth, n_phys, PAGE_SIZE, DSA_KV_HEADS * DSA_HEAD_DIM)
    zero_state = jnp.zeros((nb, GLA_HEADS, GLA_DK, GLA_DV), F32)

    xp, xs = x_prompt, x_sample
    new_p, new_s = [], []
    for l in range(depth):
        lw = _prep_layer(w_in[l], g_q[l], g_kv[l], w_uq[l], w_ukv[l], w_gla_a[l], b_gla_a[l], g_gla[l],
                         w_branch[l], w_o[l])
        mod_p = mod_all[l, :nb].reshape(nb, 9, D_MODEL)
        mod_s = mod_all[l, nb:].reshape(ndec, 9, D_MODEL)

        def prompt_mixer(pp):
            oa = _mla_prompt_call(pp['qcat'], pp['kcat'], lw['wuv_pad'])
            ob = _dsa_prompt_call(pp, consts)
            oc, st = _gla_call(pp['qc'], pp['kc'], pp['vc'], pp['la'], zero_state, consts['ones'])
            return oa, ob, oc, st

        def sample_mixer(ps):
            oa, scores = _smp1_call(l, ps, page_table, cache_mla_latent, cache_mla_krope, cache_dsa_idx_k,
                                    consts, lw['wuv_pad'])
            ob = _smp2_call(l, ps, scores, page_table, k_pool, v_pool, consts)
            oc, st = _gla_call(ps['qc'], ps['kc'], ps['vc'], ps['la'], state_gla[l], consts['ones'])
            return oa, ob, oc, st

        xp, st_p = _trunk_layer(xp, mod_p, norm_g[l], ffw, l, lw, tabs_p, prompt_mixer)
        xs, st_s = _trunk_layer(xs, mod_s, norm_g[l], ffw, l, lw, tabs_s, sample_mixer)
        new_p.append(st_p)
        new_s.append(st_s)

    lat_p, krope_p, k_p, v_p, ik_p, gla_p = [jnp.stack(a) for a in zip(*new_p)]
    lat_s, krope_s, k_s, v_s, ik_s, gla_s = [jnp.stack(a) for a in zip(*new_s)]
    return (xp, xs, lat_p, lat_s, krope_p, krope_s, k_p, k_s, v_p, v_s, ik_p, ik_s, gla_p, gla_s)
```

```python
import functools

import numpy as np
import jax
import jax.numpy as jnp
from jax import lax
from jax.experimental import pallas as pl
from jax.experimental.pallas import tpu as pltpu

F32 = jnp.float32
BF16 = jnp.bfloat16

D_MODEL = 1024
PAGE_SIZE = 128
MLA_HEADS = 8
MLA_Q_LORA = 256
MLA_KV_LORA = 128
MLA_NOPE = 64
MLA_ROPE = 32
MLA_V = 64
MLA_SCALE = (MLA_NOPE + MLA_ROPE) ** -0.5
DSA_HEADS = 8
DSA_KV_HEADS = 2
DSA_HEAD_DIM = 64
DSA_SCALE = DSA_HEAD_DIM ** -0.5
IDX_HEADS = 4
IDX_DIM = 64
IDX_SCALE = (IDX_HEADS * IDX_DIM) ** -0.5
IDX_TOPK_MAX = 256
GLA_HEADS = 4
GLA_DK = 64
GLA_DV = 128
GLA_GATE_RANK = 16
GLA_TAU = 16.0
GLA_CHUNK = 16
D_FF = 2816
N_BRANCH = 3
BRANCH_W = 512
FFN_RESID = 0.5
ROPE_THETA = 10000.0
EPS = 1e-6

LANES = 128
SUBLANES = 8
VMEM_LIMIT = 56 * 1024 * 1024
INT_MIN = -2 ** 31

_IN_SPLITS = (
    ('mla_cq', 256), ('mla_ckv', 128), ('mla_krope', 32), ('dsa_q', 512), ('dsa_k', 128), ('dsa_v', 128),
    ('idx_q', 256), ('idx_k', 64), ('idx_w', 4), ('gla_q', 256), ('gla_k', 256), ('gla_v', 512),
    ('gla_a', 16), ('gla_g', 512), ('merge', 3072),
)
_IN_OFF = {}
_o = 0
for _nm, _w in _IN_SPLITS:
    _IN_OFF[_nm] = (_o, _o + _w)
    _o += _w


def _cparams(*sem):
    return pltpu.CompilerParams(dimension_semantics=sem, vmem_limit_bytes=VMEM_LIMIT)


def _dot(a, b):
    return jnp.dot(a, b, preferred_element_type=F32)


def _dot_nt(a, b):
    return lax.dot_general(a, b, (((1,), (1,)), ((), ())), preferred_element_type=F32)


def _sigmoid(x):
    return 1.0 / (1.0 + jnp.exp(-x))


def _silu(x):
    return x * _sigmoid(x)


def _log_sigmoid(x):
    return -(jnp.maximum(-x, 0.0) + jnp.log(1.0 + jnp.exp(-jnp.abs(x))))


def _rms(x, g):
    return x * lax.rsqrt(jnp.mean(x * x, axis=-1, keepdims=True) + EPS) * g


def _rope(x, cos, sin_signed, half):
    w = x.shape[-1]
    lane = lax.broadcasted_iota(jnp.int32, x.shape, 1)
    first = (lane % (2 * half)) < half
    rot = jnp.where(first, pltpu.roll(x, w - half, 1), pltpu.roll(x, half, 1))
    return x * cos + rot * sin_signed


def _tok_tiling(n, t, tm):
    if t >= tm:
        assert t % tm == 0
        return 1, tm, t // tm
    assert tm % t == 0 and n % (tm // t) == 0
    return tm // t, t, 1


def _tok_spec(sb, tt, tps, width):
    return pl.BlockSpec((sb, tt, width), lambda i, *_: (i // tps, i % tps, 0))


def _seq_spec(sb, tps, rows, width):
    return pl.BlockSpec((sb, rows, width), lambda i, *_: (i // tps, 0, 0))


def _tab_spec(sb, tt, tps, width):
    return pl.BlockSpec((sb * tt, width), lambda i, *_: (i % tps, 0))


def _const_spec(shape):
    nd = len(shape)
    return pl.BlockSpec(shape, lambda *_: (0,) * nd)


def _mod_kernel(c_ref, w_ref, b_ref, o_ref):
    a = _silu(c_ref[...]).astype(BF16)
    o_ref[0] = _dot(a, w_ref[0]) + b_ref[0]


def _mod_call(c_all, w_mod16, b_mod):
    depth = w_mod16.shape[0]
    nc = c_all.shape[0]
    ncol = w_mod16.shape[2]
    tn = 1024
    return pl.pallas_call(
        _mod_kernel,
        grid=(depth, ncol // tn),
        in_specs=[pl.BlockSpec((nc, D_MODEL), lambda l, j: (0, 0)),
                  pl.BlockSpec((1, D_MODEL, tn), lambda l, j: (l, 0, j)),
                  pl.BlockSpec((1, 1, tn), lambda l, j: (l, 0, j))],
        out_specs=pl.BlockSpec((1, nc, tn), lambda l, j: (l, 0, j)),
        out_shape=jax.ShapeDtypeStruct((depth, nc, ncol), F32),
        compiler_params=_cparams("arbitrary", "arbitrary"),
        name="adaln_mod",
    )(c_all, w_mod16, b_mod.reshape(depth, 1, ncol))


def _ffn_kernel(x_ref, mod_ref, g_ref, wg_ref, wu_ref, wd_ref, *rest, sub, emit_h):
    if emit_h:
        o_ref, h_ref, hn_ref, acc_ref = rest
    else:
        o_ref, hn_ref, acc_ref = rest
    k = pl.program_id(1)
    sb, tt, d = x_ref.shape

    @pl.when(k == 0)
    def _():
        x = x_ref[...]
        shift = mod_ref[:, 3 * sub:3 * sub + 1, :]
        scale = mod_ref[:, 3 * sub + 1:3 * sub + 2, :]
        h = _rms(x, g_ref[2 * sub:2 * sub + 1, :]) * (1.0 + scale) + shift
        hn_ref[...] = h.reshape(sb * tt, d).astype(BF16)
        acc_ref[...] = jnp.zeros_like(acc_ref)

    hn = hn_ref[...]
    gte = _dot(hn, wg_ref[...])
    up = _dot(hn, wu_ref[...])
    act = (_silu(gte) * up).astype(BF16)
    acc_ref[...] += _dot(act, wd_ref[...])

    @pl.when(k == pl.num_programs(1) - 1)
    def _():
        y = acc_ref[...].reshape(sb, tt, d)
        gate = mod_ref[:, 3 * sub + 2:3 * sub + 3, :]
        out = x_ref[...] + FFN_RESID * (gate * _rms(y, g_ref[2 * sub + 1:2 * sub + 2, :]))
        o_ref[...] = out
        if emit_h:
            hm = _rms(out, g_ref[2:3, :]) * (1.0 + mod_ref[:, 4:5, :]) + mod_ref[:, 3:4, :]
            h_ref[...] = hm.astype(BF16)


def _ffn_call(x, mod, norm_g, wg16, wu16, wd16, layer, which, sub, emit_h, tm=512, ck=1408):
    n, t, d = x.shape
    sb, tt, tps = _tok_tiling(n, t, tm)
    grid = ((n // sb) * tps, D_FF // ck)
    out_shape = [jax.ShapeDtypeStruct((n, t, d), F32)]
    out_specs = [_tok_spec(sb, tt, tps, d)]
    if emit_h:
        out_shape.append(jax.ShapeDtypeStruct((n, t, d), BF16))
        out_specs.append(_tok_spec(sb, tt, tps, d))
    res = pl.pallas_call(
        functools.partial(_ffn_kernel, sub=sub, emit_h=emit_h),
        grid=grid,
        in_specs=[_tok_spec(sb, tt, tps, d),
                  _seq_spec(sb, tps, 9, d),
                  pl.BlockSpec((6, d), lambda i, k: (0, 0)),
                  pl.BlockSpec((None, None, d, ck), lambda i, k: (layer, which, 0, k)),
                  pl.BlockSpec((None, None, d, ck), lambda i, k: (layer, which, 0, k)),
                  pl.BlockSpec((None, None, ck, d), lambda i, k: (layer, which, k, 0))],
        out_specs=out_specs,
        out_shape=out_shape,
        scratch_shapes=[pltpu.VMEM((sb * tt, d), BF16), pltpu.VMEM((sb * tt, d), F32)],
        compiler_params=_cparams("parallel", "arbitrary"),
        name="ffn",
    )(x, mod, norm_g, wg16, wu16, wd16)
    return res if emit_h else res[0]


def _proj_mla_kernel(h_ref, wa_ref, gq_ref, gkv_ref, wuq_ref, wabs_ref, sel_ref,
                     cq_ref, sq_ref, ck_ref, sk_ref, qcat_ref, kcat_ref, ckv_ref, krope_ref):
    sb, tt, d = h_ref.shape
    tm = sb * tt
    z = _dot(h_ref[...].reshape(tm, d), wa_ref[...])
    cq = _rms(z[:, :MLA_Q_LORA], gq_ref[...])
    qh = _dot(cq.astype(BF16), wuq_ref[...])
    qn = (qh[:, :512] * MLA_SCALE).astype(BF16)
    qr = (_rope(qh[:, 512:], cq_ref[...], sq_ref[...], MLA_ROPE // 2) * MLA_SCALE).astype(BF16)
    qcat = _dot(qn, wabs_ref[...]) + _dot(qr, sel_ref[...])
    qcat_ref[...] = qcat.astype(BF16).reshape(sb, tt, qcat.shape[-1])
    ckv = _rms(z[:, 256:384], gkv_ref[...])
    kr = _rope(z[:, 384:512], ck_ref[...], sk_ref[...], MLA_ROPE // 2)
    ckv_ref[...] = ckv.reshape(sb, tt, MLA_KV_LORA)
    krope_ref[...] = kr[:, :MLA_ROPE].reshape(sb, tt, MLA_ROPE)
    kcat_ref[...] = jnp.concatenate([ckv, kr], axis=1).astype(BF16).reshape(sb, tt, 256)


def _proj_dsa_kernel(h_ref, wb_ref, c_ref, s_ref, ckw_ref, skw_ref,
                     qb_ref, kb_ref, vb_ref, qi_ref, kw_ref, ki_ref):
    sb, tt, d = h_ref.shape
    tm = sb * tt
    z = _dot(h_ref[...].reshape(tm, d), wb_ref[...])
    c1, s1 = c_ref[...], s_ref[...]
    c2, s2 = jnp.concatenate([c1, c1], axis=1), jnp.concatenate([s1, s1], axis=1)
    c4, s4 = jnp.concatenate([c2, c2], axis=1), jnp.concatenate([s2, s2], axis=1)
    half = DSA_HEAD_DIM // 2
    qb = _rope(z[:, :512], c4, s4, half) * DSA_SCALE
    qb_ref[...] = qb.astype(BF16).reshape(sb, tt, 512)
    kb_ref[...] = _rope(z[:, 512:640], c1, s1, half).reshape(sb, tt, 128)
    vb_ref[...] = z[:, 640:768].reshape(sb, tt, 128)
    qi_ref[...] = _rope(z[:, 768:1024], c2, s2, half).reshape(sb, tt, 256)
    kw = _rope(z[:, 1024:1152], ckw_ref[...], skw_ref[...], half)
    kw_ref[...] = kw.reshape(sb, tt, 128)
    ki_ref[...] = kw[:, :IDX_DIM].reshape(sb, tt, IDX_DIM)


def _proj_gla_kernel(h_ref, wc_ref, wga_ref, bga_ref, qc_ref, kc_ref, vc_ref, la_ref, gg_ref):
    sb, tt, d = h_ref.shape
    tm = sb * tt
    z = _dot(h_ref[...].reshape(tm, d), wc_ref[...])
    qc_ref[...] = (z[:, :256] * (GLA_DK ** -0.5)).reshape(sb, tt, 256)
    kc_ref[...] = z[:, 256:512].reshape(sb, tt, 256)
    vc_ref[...] = z[:, 512:1024].reshape(sb, tt, 512)
    gg_ref[...] = z[:, 1024:1536].reshape(sb, tt, 512)
    logit = _dot(z[:, 1536:1664].astype(BF16), wga_ref[...]) + bga_ref[...]
    la_ref[...] = (_log_sigmoid(logit) / GLA_TAU).reshape(sb, tt, 256)


def _proj_calls(h, lw, tabs, tm=512):
    n, t, d = h.shape
    sb, tt, tps = _tok_tiling(n, t, tm)
    grid = ((n // sb) * tps,)
    tok = lambda w: _tok_spec(sb, tt, tps, w)
    tab = lambda w: _tab_spec(sb, tt, tps, w)
    shp = lambda w, dt=F32: jax.ShapeDtypeStruct((n, t, w), dt)
    cp = _cparams("parallel")

    qcat, kcat, ckv, krope = pl.pallas_call(
        _proj_mla_kernel, grid=grid,
        in_specs=[tok(d), _const_spec((d, 512)), _const_spec((1, 256)), _const_spec((1, 128)),
                  _const_spec((256, 768)), _const_spec((512, 2048)), _const_spec((256, 2048)),
                  tab(256), tab(256), tab(128), tab(128)],
        out_specs=[tok(2048), tok(256), tok(128), tok(32)],
        out_shape=[shp(2048, BF16), shp(256, BF16), shp(128), shp(32)],
        compiler_params=cp, name="proj_mla",
    )(h, lw['wa'], lw['g_q'], lw['g_kv'], lw['w_uq'], lw['wabs'], lw['sel'],
      tabs['cos32q'], tabs['sin32q'], tabs['cos32k'], tabs['sin32k'])

    qb, kb, vb, qi, kw, ki = pl.pallas_call(
        _proj_dsa_kernel, grid=grid,
        in_specs=[tok(d), _const_spec((d, 1152)), tab(128), tab(128), tab(128), tab(128)],
        out_specs=[tok(512), tok(128), tok(128), tok(256), tok(128), tok(64)],
        out_shape=[shp(512, BF16), shp(128), shp(128), shp(256), shp(128), shp(64)],
        compiler_params=cp, name="proj_dsa",
    )(h, lw['wb'], tabs['cos64'], tabs['sin64'], tabs['cos64kw'], tabs['sin64kw'])

    qc, kc, vc, la, gg = pl.pallas_call(
        _proj_gla_kernel, grid=grid,
        in_specs=[tok(d), _const_spec((d, 1664)), _const_spec((128, 256)), _const_spec((1, 256))],
        out_specs=[tok(256), tok(256), tok(512), tok(256), tok(512)],
        out_shape=[shp(256), shp(256), shp(512), shp(256), shp(512)],
        compiler_params=cp, name="proj_gla",
    )(h, lw['wc'], lw['wga'], lw['bga'])
    return dict(qcat=qcat, kcat=kcat, ckv=ckv, krope=krope, qb=qb, kb=kb, vb=vb, qi=qi, kw=kw, ki=ki,
                qc=qc, kc=kc, vc=vc, la=la, gg=gg)


def _flash_update(s, v1, m_ref, acc_ref, g, guard):
    m_prev = m_ref[g]
    m_new = jnp.maximum(m_prev, jnp.max(s, axis=1, keepdims=True))
    m_use = jnp.where(m_new == -jnp.inf, 0.0, m_new) if guard else m_new
    alpha = jnp.exp(m_prev - m_use)
    p = jnp.exp(s - jnp.concatenate([m_use] * (s.shape[1] // LANES), axis=1))
    acc_ref[g] = jnp.concatenate([alpha, alpha], axis=1) * acc_ref[g] + _dot(p.astype(BF16), v1)
    m_ref[g] = m_new


def _mla_p_kernel(q_ref, k_ref, wuv_ref, o_ref, m_ref, acc_ref, *, tq, tk, ngrp):
    qi = pl.program_id(1)
    nh = MLA_HEADS
    hpg = nh // ngrp
    qs = [jnp.concatenate([q_ref[0, :, 256 * h:256 * (h + 1)] for h in range(g * hpg, (g + 1) * hpg)], axis=0)
          for g in range(ngrp)]
    m_ref[...] = jnp.full_like(m_ref, -jnp.inf)
    acc_ref[...] = jnp.zeros_like(acc_ref)
    ones = jnp.ones((tk, LANES), BF16)

    def step(j, mask):
        start = pl.multiple_of(j * tk, tk)
        k = k_ref[0, pl.ds(start, tk), :]
        v1 = jnp.concatenate([k[:, :MLA_KV_LORA], ones], axis=1)
        for g in range(ngrp):
            s = _dot_nt(qs[g], k)
            if mask is not None:
                s = jnp.where(mask, s, -jnp.inf)
            _flash_update(s, v1, m_ref, acc_ref, g, False)

    nfull = qi * (tq // tk)

    def body(j, c):
        step(j, None)
        return c

    lax.fori_loop(0, nfull, body, 0)
    row = lax.broadcasted_iota(jnp.int32, (tq, tk), 0)
    col = lax.broadcasted_iota(jnp.int32, (tq, tk), 1)
    for dj in range(tq // tk):
        mk = row >= col + dj * tk
        step(nfull + dj, jnp.concatenate([mk] * hpg, axis=0))

    out = jnp.zeros((tq, 512), F32)
    for g in range(ngrp):
        acc = acc_ref[g]
        o_lat = (acc[:, :MLA_KV_LORA] / acc[:, MLA_KV_LORA:]).astype(BF16)
        for hh in range(hpg):
            out = out + _dot(o_lat[hh * tq:(hh + 1) * tq], wuv_ref[g * hpg + hh])
    o_ref[0] = out.astype(BF16)


def _mla_prompt_call(qcat, kcat, wuv_pad, tq=256, tk=256, ngrp=2):
    n, t, _ = qcat.shape
    rows = MLA_HEADS // ngrp * tq
    return pl.pallas_call(
        functools.partial(_mla_p_kernel, tq=tq, tk=tk, ngrp=ngrp),
        grid=(n, t // tq),
        in_specs=[pl.BlockSpec((1, tq, 2048), lambda b, i: (b, i, 0)),
                  pl.BlockSpec((1, t, 256), lambda b, i: (b, 0, 0)),
                  _const_spec((MLA_HEADS, 128, 512))],
        out_specs=pl.BlockSpec((1, tq, 512), lambda b, i: (b, i, 0)),
        out_shape=jax.ShapeDtypeStruct((n, t, 512), BF16),
        scratch_shapes=[pltpu.VMEM((ngrp, rows, LANES), F32), pltpu.VMEM((ngrp, rows, 2 * LANES), F32)],
        compiler_params=_cparams("parallel", "arbitrary"),
        name="mla_prompt",
    )(qcat, kcat, wuv_pad)


def _sort_key(score):
    bits = lax.bitcast_convert_type(score + 0.0, jnp.int32)
    return jnp.where(bits < 0, bits ^ jnp.int32(0x7FFFFFFF), bits)


def _lane_fold(x):
    parts = [x[:, i * LANES:(i + 1) * LANES] for i in range(x.shape[1] // LANES)]
    while len(parts) > 1:
        parts = [parts[i] + parts[i + 1] if i + 1 < len(parts) else parts[i] for i in range(0, len(parts), 2)]
    return parts[0]


def _count(mask):
    return jnp.sum(_lane_fold(jnp.where(mask, 1.0, 0.0)), axis=1, keepdims=True)


def _kth_and_ties(count_fn, rows, kk, idx_bits):
    kkf = jnp.float32(kk)
    thr = jnp.where(count_fn(lambda k, i: k >= 0) >= kkf, jnp.int32(0), jnp.int32(INT_MIN))

    def vbody(b, thr):
        cand = thr | lax.shift_left(jnp.int32(1), jnp.int32(30) - b)
        return jnp.where(count_fn(lambda k, i: k >= cand) >= kkf, cand, thr)

    thr = lax.fori_loop(0, 31, vbody, thr)
    need = kkf - count_fn(lambda k, i: k > thr)
    surplus = jnp.max(count_fn(lambda k, i: k == thr) - need)

    def tie(_):
        def pbody(b, p):
            cand = p | lax.shift_left(jnp.int32(1), jnp.int32(idx_bits - 1) - b)
            return jnp.where(count_fn(lambda k, i: (k == thr) & (i < cand)) <= need, cand, p)
        return lax.fori_loop(0, idx_bits, pbody, jnp.zeros((rows, 1), jnp.int32))

    p = lax.cond(surplus > 0.0, tie, lambda _: jnp.full((rows, 1), 2 ** idx_bits - 1, jnp.int32), 0)
    return thr, p


def _topk_mask(key, kk, idx_bits):
    rows, width = key.shape
    idx = lax.broadcasted_iota(jnp.int32, (rows, width), 1)
    thr, p = _kth_and_ties(lambda pred: _count(pred(key, idx)), rows, kk, idx_bits)
    return (key > thr) | ((key == thr) & (idx < p))


def _split3_q(qf, ph_ref, plo_ref):
    q_hi = qf.astype(BF16)
    q_lo = (qf - q_hi.astype(F32)).astype(BF16)
    return jnp.concatenate(
        [_dot(q_hi, ph_ref[h]) + _dot(q_lo, plo_ref[h]) for h in range(IDX_HEADS)], axis=0).astype(BF16)


def _split3_k(kk, pkh_ref, pkl_ref):
    w = kk.shape[1]
    k_hi = kk.astype(BF16)
    k_lo = (kk - k_hi.astype(F32)).astype(BF16)
    return (_dot(k_hi, pkh_ref[:w, :]) + _dot(k_lo, pkl_ref[:w, :])).astype(BF16)


def _idx_weighted(dots, wq, tq):
    score = jnp.zeros((tq, dots.shape[1]), F32)
    for h in range(IDX_HEADS):
        wh = wq[:, IDX_DIM + h:IDX_DIM + h + 1] * IDX_SCALE
        score = score + wh * jnp.maximum(dots[h * tq:(h + 1) * tq], 0.0)
    return score


def _idx_scores(q3, k3, wq, tq):
    return _idx_weighted(_dot_nt(q3, k3), wq, tq)


def _dsa_p_kernel(qi_ref, kwq_ref, kwk_ref, qb_ref, kb_ref, vb_ref, ph_ref, plo_ref, pkh_ref, pkl_ref,
                  pq_ref, po_ref, o_ref, k3_ref, kb16_ref, vb1_ref, key_ref, m_ref, acc_ref, *, tq, ck, topk):
    it = pl.program_id(1)
    s_len = kwk_ref.shape[1]
    grp = DSA_HEADS // DSA_KV_HEADS

    @pl.when(it == 0)
    def _():
        k3_ref[...] = _split3_k(kwk_ref[0], pkh_ref, pkl_ref)
        kb16_ref[...] = kb_ref[0].astype(BF16)
        vb1_ref[...] = jnp.concatenate([vb_ref[0], jnp.ones((s_len, LANES), F32)], axis=1).astype(BF16)

    nc = lax.shift_right_logical(it * tq + (tq + ck - 1), int(ck).bit_length() - 1)
    qpos = it * tq + lax.broadcasted_iota(jnp.int32, (tq, ck), 0)
    lanes = lax.broadcasted_iota(jnp.int32, (tq, ck), 1)
    chunk_start = lambda c: pl.multiple_of(c * ck, ck)

    q3 = _split3_q(qi_ref[0], ph_ref, plo_ref)
    wq = kwq_ref[0]

    def sc_body(c, carry):
        start = chunk_start(c)
        score = _idx_scores(q3, k3_ref[pl.ds(start, ck), :], wq, tq)
        key_ref[:, pl.ds(start, ck)] = _sort_key(jnp.where(start + lanes <= qpos, score, -jnp.inf))
        return carry

    lax.fori_loop(0, nc, sc_body, 0)

    def count_fn(pred):
        def body(c, acc):
            start = chunk_start(c)
            hit = pred(key_ref[:, pl.ds(start, ck)], start + lanes)
            return acc + _lane_fold(jnp.where(hit, 1.0, 0.0))
        acc = lax.fori_loop(0, nc, body, jnp.zeros((tq, LANES), F32))
        return jnp.sum(acc, axis=1, keepdims=True)

    idx_bits = int(s_len).bit_length()
    thr, tie_end = lax.cond(
        it * tq + tq > topk,
        lambda _: _kth_and_ties(count_fn, tq, topk, idx_bits),
        lambda _: (jnp.full((tq, 1), INT_MIN, jnp.int32), jnp.full((tq, 1), 2 ** idx_bits - 1, jnp.int32)),
        0)

    m_ref[...] = jnp.full_like(m_ref, -jnp.inf)
    acc_ref[...] = jnp.zeros_like(acc_ref)
    qb = qb_ref[0]
    qrows = []
    for j in range(DSA_KV_HEADS):
        qp = _dot(qb, pq_ref[j])
        qrows.append(jnp.concatenate([qp[:, 128 * g:128 * (g + 1)] for g in range(grp)], axis=0).astype(BF16))

    def at_body(c, carry):
        start = chunk_start(c)
        kc = key_ref[:, pl.ds(start, ck)]
        pos = start + lanes
        sel = ((kc > thr) | ((kc == thr) & (pos < tie_end))) & (pos <= qpos)
        sel4 = jnp.concatenate([sel] * grp, axis=0)
        kb = kb16_ref[pl.ds(start, ck), :]
        v1 = vb1_ref[pl.ds(start, ck), :]
        for j in range(DSA_KV_HEADS):
            s = jnp.where(sel4, _dot_nt(qrows[j], kb), -jnp.inf)
            _flash_update(s, v1, m_ref, acc_ref, j, True)
        return carry

    lax.fori_loop(0, nc, at_body, 0)
    out = jnp.zeros((tq, 512), F32)
    for j in range(DSA_KV_HEADS):
        acc = acc_ref[j]
        o = (acc[:, :LANES] / acc[:, LANES:]).astype(BF16)
        for g in range(grp):
            out = out + _dot(o[g * tq:(g + 1) * tq], po_ref[j * grp + g])
    o_ref[0] = out.astype(BF16)


def _dsa_prompt_call(pr, consts, tq=256, ck=512):
    n, t, _ = pr['qi'].shape
    ck = min(ck, t)
    tq = min(tq, t)
    assert t % ck == 0 and ck & (ck - 1) == 0
    topk = min(IDX_TOPK_MAX, t // 4)
    rows = DSA_HEADS // DSA_KV_HEADS * tq
    qspec = lambda w: pl.BlockSpec((1, tq, w), lambda b, i: (b, i, 0))
    sspec = lambda w: pl.BlockSpec((1, t, w), lambda b, i: (b, 0, 0))
    return pl.pallas_call(
        functools.partial(_dsa_p_kernel, tq=tq, ck=ck, topk=topk),
        grid=(n, t // tq),
        in_specs=[qspec(256), qspec(128), sspec(128), qspec(512), sspec(128), sspec(128),
                  _const_spec((4, 256, 256)), _const_spec((4, 256, 256)), _const_spec((128, 256)),
                  _const_spec((128, 256)), _const_spec((2, 512, 512)), _const_spec((8, 128, 512))],
        out_specs=qspec(512),
        out_shape=jax.ShapeDtypeStruct((n, t, 512), BF16),
        scratch_shapes=[pltpu.VMEM((t, 256), BF16), pltpu.VMEM((t, 128), BF16), pltpu.VMEM((t, 256), BF16),
                        pltpu.VMEM((tq, t), jnp.int32),
                        pltpu.VMEM((DSA_KV_HEADS, rows, LANES), F32), pltpu.VMEM((DSA_KV_HEADS, rows, 2 * LANES), F32)],
        compiler_params=_cparams("parallel", "arbitrary"),
        name="dsa_prompt",
    )(pr['qi'], pr['kw'], pr['kw'], pr['qb'], pr['kb'], pr['vb'],
      consts['ph'], consts['pl'], consts['pkh'], consts['pkl'], consts['pq'], consts['po'])


def _gla_kernel(q_ref, k_ref, v_ref, la_ref, s0_ref, ones_ref, o_ref, sfin_ref, st_ref, *, chunk, tt):
    ti = pl.program_id(1)
    nh, dk, dv = GLA_HEADS, GLA_DK, GLA_DV
    hd = nh * dk

    @pl.when(ti == 0)
    def _():
        st_ref[...] = s0_ref[0].astype(F32).reshape(hd, dv)

    q = q_ref[0]
    k = k_ref[0]
    v = v_ref[0]
    la = la_ref[0]
    rc = lax.broadcasted_iota(jnp.int32, (tt, hd), 0) % chunk
    rcv = lax.broadcasted_iota(jnp.int32, (tt, nh * dv), 0) % chunk

    b = la
    sft = 1
    while sft < chunk:
        b = b + jnp.where(rc >= sft, pltpu.roll(b, sft, 0), 0.0)
        sft *= 2
    bl = jnp.where(rc == chunk - 1, b, 0.0)
    sft = 1
    while sft < chunk:
        bl = bl + jnp.where(rc < chunk - sft, pltpu.roll(bl, tt - sft, 0), 0.0)
        sft *= 2

    ones = ones_ref[...]
    o_intra = jnp.zeros((tt, nh * dv), F32)
    for lag in range(chunk):
        if lag == 0:
            pr = q * k
            vs = v
        else:
            valid = rc >= lag
            ks = pltpu.roll(k, lag, 0)
            bs = pltpu.roll(b, lag, 0)
            pr = jnp.where(valid, q * ks * jnp.exp(jnp.minimum(b - bs, 0.0)), 0.0)
            vs = jnp.where(rcv >= lag, pltpu.roll(v, lag, 0), 0.0)
        att = _dot(pr.astype(BF16), ones)
        o_intra = o_intra + att * vs

    qe = q * jnp.exp(b)
    kd = k * jnp.exp(bl - b)
    pad = LANES - tt
    if pad:
        kd = jnp.concatenate([kd, jnp.zeros((pad, hd), F32)], axis=0)
        vp = jnp.concatenate([v, jnp.zeros((pad, nh * dv), F32)], axis=0)
        blp = jnp.concatenate([bl, jnp.zeros((pad, hd), F32)], axis=0)
    else:
        vp, blp = v, bl
    kdt = kd.T.astype(BF16)
    dect = jnp.exp(blp.T)
    v16 = vp.astype(BF16)
    lane = lax.broadcasted_iota(jnp.int32, (hd, LANES), 1)
    rowh = lax.broadcasted_iota(jnp.int32, (nh * chunk, hd), 0) // chunk
    colh = lax.broadcasted_iota(jnp.int32, (nh * chunk, hd), 1) // dk
    headmask = rowh == colh
    st = st_ref[...]
    outs = []
    for c in range(tt // chunk):
        qe_c = qe[c * chunk:(c + 1) * chunk]
        x = jnp.where(headmask, jnp.concatenate([qe_c] * nh, axis=0), 0.0).astype(BF16)
        r = _dot(x, st.astype(BF16))
        outs.append(jnp.concatenate([r[h * chunk:(h + 1) * chunk] for h in range(nh)], axis=1))
        inchunk = (lane >= c * chunk) & (lane < (c + 1) * chunk)
        u = _dot(jnp.where(inchunk, kdt, jnp.zeros_like(kdt)), v16)
        upd = jnp.concatenate([u[h * dk:(h + 1) * dk, h * dv:(h + 1) * dv] for h in range(nh)], axis=0)
        last = (c + 1) * chunk - 1
        st = dect[:, last:last + 1] * st + upd
    st_ref[...] = st
    o_inter = outs[0] if len(outs) == 1 else jnp.concatenate(outs, axis=0)
    o_ref[0] = o_inter + o_intra

    @pl.when(ti == pl.num_programs(1) - 1)
    def _():
        sfin_ref[0] = st.reshape(nh, dk, dv)


def _gla_call(qc, kc, vc, la, s0, ones_blk):
    n, t, _ = qc.shape
    chunk = int(np.gcd(t, GLA_CHUNK))
    tt = min(t, LANES)
    assert t % tt == 0 and tt % chunk == 0
    tspec = lambda w: pl.BlockSpec((1, tt, w), lambda b, i: (b, i, 0))
    sspec = pl.BlockSpec((1, GLA_HEADS, GLA_DK, GLA_DV), lambda b, i: (b, 0, 0, 0))
    return pl.pallas_call(
        functools.partial(_gla_kernel, chunk=chunk, tt=tt),
        grid=(n, t // tt),
        in_specs=[tspec(256), tspec(256), tspec(512), tspec(256), sspec, _const_spec((256, 512))],
        out_specs=[tspec(512), sspec],
        out_shape=[jax.ShapeDtypeStruct((n, t, 512), F32),
                   jax.ShapeDtypeStruct((n, GLA_HEADS, GLA_DK, GLA_DV), F32)],
        scratch_shapes=[pltpu.VMEM((GLA_HEADS * GLA_DK, GLA_DV), F32)],
        compiler_params=_cparams("parallel", "arbitrary"),
        name="gla",
    )(qc, kc, vc, la, s0, ones_blk)


def _mix_kernel(h_ref, x_ref, mod_ref, g_ref, oa_ref, ob_ref, oc_ref, gg_ref, ggla_ref,
                wm_ref, wbr_ref, wo_ref, o_ref):
    sb, tt, d = x_ref.shape
    tm = sb * tt
    h = h_ref[...].reshape(tm, d)
    gates = _sigmoid(_dot(h, wm_ref[...]))
    oc = oc_ref[...].reshape(tm, BRANCH_W)
    gg = gg_ref[...].reshape(tm, BRANCH_W)
    ggla = ggla_ref[...]
    parts = []
    for hh in range(GLA_HEADS):
        seg = oc[:, GLA_DV * hh:GLA_DV * (hh + 1)]
        parts.append(_rms(seg, ggla))
    ocn = (jnp.concatenate(parts, axis=1) * _silu(gg)).astype(BF16)
    ya = _dot(oa_ref[...].reshape(tm, BRANCH_W), wbr_ref[0])
    yb = _dot(ob_ref[...].reshape(tm, BRANCH_W), wbr_ref[1])
    yc = _dot(ocn, wbr_ref[2])
    y = gates[:, :d] * ya + gates[:, d:2 * d] * yb + gates[:, 2 * d:] * yc
    out = _dot(y.astype(BF16), wo_ref[...]).reshape(sb, tt, d)
    o_ref[...] = x_ref[...] + mod_ref[:, 5:6, :] * _rms(out, g_ref[3:4, :])


def _mix_call(h, x, mod, norm_g, oa, ob, oc, gg, lw, tm=256):
    n, t, d = x.shape
    sb, tt, tps = _tok_tiling(n, t, tm)
    tok = lambda w: _tok_spec(sb, tt, tps, w)
    return pl.pallas_call(
        _mix_kernel,
        grid=((n // sb) * tps,),
        in_specs=[tok(d), tok(d), _seq_spec(sb, tps, 9, d), _const_spec((6, d)),
                  tok(512), tok(512), tok(512), tok(512), _const_spec((1, 128)),
                  _const_spec((d, 3 * d)), _const_spec((3, 512, d)), _const_spec((d, d))],
        out_specs=tok(d),
        out_shape=jax.ShapeDtypeStruct((n, t, d), F32),
        compiler_params=_cparams("parallel"),
        name="mix_out",
    )(h, x, mod, norm_g, oa, ob, oc, gg, lw['g_gla'], lw['wm'], lw['wbr'], lw['wo'])


def _smp1_kernel(pt_ref, q_ref, qi_ref, kwq_ref, kcn_ref, kwn_ref, ph_ref, plo_ref, pkh_ref, pkl_ref, wuv_ref,
                 *rest, pps, t_new):
    lat = rest[:pps]
    kro = rest[pps:2 * pps]
    ikp = rest[2 * pps:3 * pps]
    o_ref, sc_ref, m_ref, l_ref, acc_ref = rest[3 * pps:]
    j = pl.program_id(1)
    nj = pl.num_programs(1)
    nh = MLA_HEADS
    tq = t_new

    @pl.when(j == 0)
    def _():
        m_ref[...] = jnp.full_like(m_ref, -jnp.inf)
        l_ref[...] = jnp.zeros_like(l_ref)
        acc_ref[...] = jnp.zeros_like(acc_ref)

    qall = q_ref[0].astype(F32)
    qa = jnp.concatenate([qall[:, 256 * h:256 * h + 128] for h in range(nh)], axis=0).astype(BF16)
    qr = jnp.concatenate([qall[:, 256 * h + 128:256 * h + 256] for h in range(nh)], axis=0).astype(BF16)
    q3 = _split3_q(qi_ref[0], ph_ref, plo_ref)
    wq = kwq_ref[0]

    def idx_scores(kslab):
        return _idx_scores(q3, _split3_k(kslab, pkh_ref, pkl_ref), wq, tq)

    def flash(s, vals):
        m_prev = m_ref[...]
        m_new = jnp.maximum(m_prev, jnp.max(s, axis=1, keepdims=True))
        alpha = jnp.exp(m_prev - m_new)
        p = jnp.exp(s - m_new)
        l_ref[...] = alpha * l_ref[...] + jnp.sum(p, axis=1, keepdims=True)
        acc_ref[...] = alpha * acc_ref[...] + _dot(p.astype(BF16), vals)
        m_ref[...] = m_new

    @pl.when(j < nj - 1)
    def _():
        c = jnp.concatenate([lat[i][...] for i in range(pps)], axis=0).astype(BF16)
        rt = jnp.concatenate([kro[i][...] for i in range(pps)], axis=1).astype(BF16)
        flash(_dot_nt(qa, c) + _dot(qr[:, :MLA_ROPE], rt), c)
        kt = jnp.concatenate([ikp[i][...] for i in range(pps)], axis=1)
        k_hi = kt.astype(BF16)
        k_lo = (kt - k_hi.astype(F32)).astype(BF16)
        k3t = jnp.concatenate([k_hi, k_lo, k_hi, jnp.zeros_like(k_hi)], axis=0)
        sc_ref[0] = _idx_weighted(_dot(q3, k3t), wq, tq)

    @pl.when(j == nj - 1)
    def _():
        kc = jnp.concatenate([kcn_ref[0].astype(F32), jnp.zeros((LANES - tq, 256), F32)], axis=0).astype(BF16)
        s = _dot_nt(jnp.concatenate([qa, qr], axis=1), kc)
        row = lax.broadcasted_iota(jnp.int32, (nh * tq, LANES), 0) % tq
        col = lax.broadcasted_iota(jnp.int32, (nh * tq, LANES), 1)
        flash(jnp.where(col <= row, s, -jnp.inf), kc[:, :MLA_KV_LORA])
        o_lat = acc_ref[...] / l_ref[...]
        out = jnp.zeros((tq, 512), F32)
        for h in range(nh):
            out = out + _dot(o_lat[h * tq:(h + 1) * tq].astype(BF16), wuv_ref[h])
        o_ref[0] = out.astype(BF16)
        kslab = jnp.concatenate([kwn_ref[0], jnp.zeros((LANES - tq, LANES), F32)], axis=0)
        sc = idx_scores(kslab)
        r8 = lax.broadcasted_iota(jnp.int32, (tq, LANES), 0)
        c8 = lax.broadcasted_iota(jnp.int32, (tq, LANES), 1)
        tail = jnp.where(c8 <= r8, sc, -jnp.inf)
        fill = jnp.full((tq, pps * PAGE_SIZE - LANES), -jnp.inf, F32)
        sc_ref[0] = jnp.concatenate([tail, fill], axis=1)


def _page_spec(layer, i, pps, n_pages, rows, cols):
    return pl.BlockSpec((None, None, rows, cols),
                        lambda b, j, pt: (layer, pt[b, jnp.minimum(j * pps + i, n_pages - pps + i)], 0, 0))


def _smp1_call(layer, pr, page_table, lat_pool, krt_pool, ikt_pool, consts, wuv_pad, pps=16):
    n, t, _ = pr['qcat'].shape
    n_pages = page_table.shape[1]
    pps = min(pps, n_pages)
    assert n_pages % pps == 0
    nj = n_pages // pps + 1
    width = nj * pps * PAGE_SIZE
    page_spec = lambda i, rows, cols: _page_spec(layer, i, pps, n_pages, rows, cols)

    nspec = lambda w: pl.BlockSpec((1, t, w), lambda b, j, pt: (b, 0, 0))
    grid_spec = pltpu.PrefetchScalarGridSpec(
        num_scalar_prefetch=1,
        grid=(n, nj),
        in_specs=[nspec(2048), nspec(256), nspec(128), nspec(256), nspec(128),
                  _const_spec((4, 256, 256)), _const_spec((4, 256, 256)), _const_spec((128, 256)),
                  _const_spec((128, 256)), _const_spec((MLA_HEADS, 128, 512))]
                 + [page_spec(i, PAGE_SIZE, MLA_KV_LORA) for i in range(pps)]
                 + [page_spec(i, MLA_ROPE, PAGE_SIZE) for i in range(pps)]
                 + [page_spec(i, IDX_DIM, PAGE_SIZE) for i in range(pps)],
        out_specs=[pl.BlockSpec((1, t, 512), lambda b, j, pt: (b, 0, 0)),
                   pl.BlockSpec((1, t, pps * PAGE_SIZE), lambda b, j, pt: (b, 0, j))],
        scratch_shapes=[pltpu.VMEM((MLA_HEADS * t, 1), F32), pltpu.VMEM((MLA_HEADS * t, 1), F32),
                        pltpu.VMEM((MLA_HEADS * t, MLA_KV_LORA), F32)],
    )
    return pl.pallas_call(
        functools.partial(_smp1_kernel, pps=pps, t_new=t),
        grid_spec=grid_spec,
        out_shape=[jax.ShapeDtypeStruct((n, t, 512), BF16), jax.ShapeDtypeStruct((n, t, width), F32)],
        compiler_params=_cparams("parallel", "arbitrary"),
        name="mla_idx_sample",
    )(page_table, pr['qcat'], pr['qi'], pr['kw'], pr['kcat'], pr['kw'],
      consts['ph'], consts['pl'], consts['pkh'], consts['pkl'], wuv_pad,
      *([lat_pool] * pps), *([krt_pool] * pps), *([ikt_pool] * pps))


def _smp2_kernel(pt_ref, sc_ref, qb_ref, kbn_ref, vbn_ref, pq_ref, po_ref, *rest, pps, t_new, topk):
    kp = rest[:pps]
    vp = rest[pps:2 * pps]
    o_ref, mask_ref, q_scr, m_ref, l_ref, acc_ref = rest[2 * pps:]
    j = pl.program_id(1)
    nj = pl.num_programs(1)
    tq = t_new
    grp = DSA_HEADS // DSA_KV_HEADS
    nrow = DSA_KV_HEADS * grp * tq
    width = sc_ref.shape[2]
    cw = pps * PAGE_SIZE

    @pl.when(j == 0)
    def _():
        key = _sort_key(sc_ref[0])
        sel = _topk_mask(key, topk, int(width).bit_length())
        mask_ref[...] = jnp.where(sel & (sc_ref[0] > -jnp.inf), 1.0, 0.0)
        m_ref[...] = jnp.full_like(m_ref, -jnp.inf)
        l_ref[...] = jnp.zeros_like(l_ref)
        acc_ref[...] = jnp.zeros_like(acc_ref)
        qb = qb_ref[0]
        rows = []
        for jj in range(DSA_KV_HEADS):
            qp = _dot(qb, pq_ref[jj])
            rows += [qp[:, 128 * g:128 * (g + 1)] for g in range(grp)]
        q_scr[...] = jnp.concatenate(rows, axis=0).astype(BF16)

    def flash(s, pv):
        m_prev = m_ref[...]
        m_new = jnp.maximum(m_prev, jnp.max(s, axis=1, keepdims=True))
        m_safe = jnp.where(m_new == -jnp.inf, 0.0, m_new)
        alpha = jnp.exp(m_prev - m_safe)
        p = jnp.exp(s - m_safe)
        l_ref[...] = alpha * l_ref[...] + jnp.sum(p, axis=1, keepdims=True)
        acc_ref[...] = alpha * acc_ref[...] + pv(p.astype(BF16))
        m_ref[...] = m_new

    q = q_scr[...]
    start = pl.multiple_of(j * cw, cw)
    msk = mask_ref[:, pl.ds(start, cw)]
    msk = jnp.concatenate([msk] * (DSA_KV_HEADS * grp), axis=0) > 0.5

    @pl.when(j < nj - 1)
    def _():
        kt = jnp.concatenate([kp[i][...] for i in range(pps)], axis=1).astype(BF16)
        vt = jnp.concatenate([vp[i][...] for i in range(pps)], axis=1).astype(BF16)
        flash(jnp.where(msk, _dot(q, kt), -jnp.inf), lambda p: _dot_nt(p, vt))

    @pl.when(j == nj - 1)
    def _():
        zr = jnp.zeros((LANES - tq, LANES), F32)
        k = jnp.concatenate([kbn_ref[0], zr], axis=0).astype(BF16)
        v = jnp.concatenate([vbn_ref[0], zr], axis=0).astype(BF16)
        flash(jnp.where(msk[:, :LANES], _dot_nt(q, k), -jnp.inf), lambda p: _dot(p, v))
        o = acc_ref[...] / l_ref[...]
        out = jnp.zeros((tq, 512), F32)
        for jj in range(DSA_KV_HEADS):
            for g in range(grp):
                r0 = (jj * grp + g) * tq
                out = out + _dot(o[r0:r0 + tq].astype(BF16), po_ref[jj * grp + g])
        o_ref[0] = out.astype(BF16)


def _smp2_call(layer, pr, scores, page_table, kt_pool, vt_pool, consts, pps=16):
    n, t, _ = pr['qb'].shape
    n_pages = page_table.shape[1]
    pps = min(pps, n_pages)
    nj = n_pages // pps + 1
    width = scores.shape[2]
    assert width == nj * pps * PAGE_SIZE
    past = n_pages * PAGE_SIZE
    topk = min(IDX_TOPK_MAX, (past + t) // 4)
    nrow = DSA_HEADS * t
    page_spec = lambda i: _page_spec(layer, i, pps, n_pages, DSA_KV_HEADS * DSA_HEAD_DIM, PAGE_SIZE)

    nspec = lambda w: pl.BlockSpec((1, t, w), lambda b, j, pt: (b, 0, 0))
    grid_spec = pltpu.PrefetchScalarGridSpec(
        num_scalar_prefetch=1,
        grid=(n, nj),
        in_specs=[nspec(width), nspec(512), nspec(128), nspec(128),
                  _const_spec((2, 512, 512)), _const_spec((8, 128, 512))]
                 + [page_spec(i) for i in range(pps)] + [page_spec(i) for i in range(pps)],
        out_specs=pl.BlockSpec((1, t, 512), lambda b, j, pt: (b, 0, 0)),
        scratch_shapes=[pltpu.VMEM((t, width), F32), pltpu.VMEM((nrow, 128), BF16),
                        pltpu.VMEM((nrow, 1), F32), pltpu.VMEM((nrow, 1), F32), pltpu.VMEM((nrow, 128), F32)],
    )
    return pl.pallas_call(
        functools.partial(_smp2_kernel, pps=pps, t_new=t, topk=topk),
        grid_spec=grid_spec,
        out_shape=jax.ShapeDtypeStruct((n, t, 512), BF16),
        compiler_params=_cparams("parallel", "arbitrary"),
        name="dsa_sample",
    )(page_table, scores, pr['qb'], pr['kb'], pr['vb'], consts['pq'], consts['po'],
      *([kt_pool] * pps), *([vt_pool] * pps))


def _placement_consts():
    ph = np.zeros((4, 256, 256), np.float32)
    plo = np.zeros((4, 256, 256), np.float32)
    for h in range(4):
        for dd in range(64):
            ph[h, 64 * h + dd, dd] = 1.0
            ph[h, 64 * h + dd, 64 + dd] = 1.0
            plo[h, 64 * h + dd, 128 + dd] = 1.0
    pkh = np.zeros((128, 256), np.float32)
    pkl = np.zeros((128, 256), np.float32)
    for dd in range(64):
        pkh[dd, dd] = 1.0
        pkh[dd, 128 + dd] = 1.0
        pkl[dd, 64 + dd] = 1.0
    pq = np.zeros((2, 512, 512), np.float32)
    po = np.zeros((8, 128, 512), np.float32)
    for j in range(2):
        for g in range(4):
            head = 4 * j + g
            for e in range(64):
                pq[j, 64 * head + e, 128 * g + 64 * j + e] = 1.0
                po[head, 64 * j + e, 64 * head + e] = 1.0
    ones = np.zeros((256, 512), np.float32)
    for h in range(4):
        ones[64 * h:64 * (h + 1), 128 * h:128 * (h + 1)] = 1.0
    mk = lambda a: jnp.asarray(a, BF16)
    return dict(ph=mk(ph), pl=mk(plo), pkh=mk(pkh), pkl=mk(pkl), pq=mk(pq), po=mk(po), ones=mk(ones))


def _rope_tables(pos, rep):
    def base(half):
        inv = ROPE_THETA ** (-jnp.arange(half, dtype=F32) / half)
        ang = pos.astype(F32)[:, None] * inv[None, :]
        c, s = jnp.cos(ang), jnp.sin(ang)
        return jnp.concatenate([c, c], axis=1), jnp.concatenate([-s, s], axis=1)

    c32, s32 = base(16)
    c64, s64 = base(32)
    tpos = pos.shape[0]
    one = lambda w: jnp.ones((tpos, w), F32)
    zero = lambda w: jnp.zeros((tpos, w), F32)
    tabs = dict(
        cos32q=jnp.tile(c32, (1, 8)), sin32q=jnp.tile(s32, (1, 8)),
        cos32k=jnp.concatenate([c32, one(96)], axis=1), sin32k=jnp.concatenate([s32, zero(96)], axis=1),
        cos64=jnp.tile(c64, (1, 2)), sin64=jnp.tile(s64, (1, 2)),
        cos64kw=jnp.concatenate([c64, one(64)], axis=1), sin64kw=jnp.concatenate([s64, zero(64)], axis=1),
    )
    if rep > 1:
        tabs = {k: jnp.tile(v, (rep, 1)) for k, v in tabs.items()}
    return tabs


def _prep_layer(w_in, g_q, g_kv, w_uq, w_ukv, w_gla_a, b_gla_a, g_gla, w_branch, w_o):
    col = lambda nm: w_in[:, _IN_OFF[nm][0]:_IN_OFF[nm][1]]
    zc = lambda w: jnp.zeros((D_MODEL, w), w_in.dtype)
    wa = jnp.concatenate([col('mla_cq'), col('mla_ckv'), col('mla_krope'), zc(96)], axis=1)
    wb = jnp.concatenate([col('dsa_q'), col('dsa_k'), col('dsa_v'), col('idx_q'), col('idx_k'), col('idx_w'),
                          zc(60)], axis=1)
    wc = jnp.concatenate([col('gla_q'), col('gla_k'), col('gla_v'), col('gla_g'), col('gla_a'), zc(112)], axis=1)
    wm = col('merge')
    uq = w_uq.reshape(MLA_Q_LORA, MLA_HEADS, MLA_NOPE + MLA_ROPE)
    w_uq_r = jnp.concatenate([uq[:, :, :MLA_NOPE].reshape(MLA_Q_LORA, -1),
                              uq[:, :, MLA_NOPE:].reshape(MLA_Q_LORA, -1)], axis=1)
    w_uk = w_ukv[:, :, :MLA_NOPE]
    w_uv = w_ukv[:, :, MLA_NOPE:]
    wabs = jnp.zeros((MLA_HEADS * MLA_NOPE, MLA_HEADS * 256), w_in.dtype)
    wuv_pad = jnp.zeros((MLA_HEADS, MLA_KV_LORA, 512), w_in.dtype)
    sel = np.zeros((MLA_HEADS * MLA_ROPE, MLA_HEADS * 256), np.float32)
    for h in range(MLA_HEADS):
        wabs = wabs.at[MLA_NOPE * h:MLA_NOPE * (h + 1), 256 * h:256 * h + MLA_KV_LORA].set(w_uk[:, h, :].T)
        wuv_pad = wuv_pad.at[h, :, MLA_V * h:MLA_V * (h + 1)].set(w_uv[:, h, :])
        for dd in range(MLA_ROPE):
            sel[MLA_ROPE * h + dd, 256 * h + MLA_KV_LORA + dd] = 1.0
    wga = jnp.concatenate([w_gla_a, jnp.zeros((LANES - GLA_GATE_RANK, w_gla_a.shape[1]), w_gla_a.dtype)], axis=0)
    b16 = lambda a: a.astype(BF16)
    return dict(wa=b16(wa), wb=b16(wb), wc=b16(wc), wm=b16(wm), w_uq=b16(w_uq_r), wabs=b16(wabs),
                sel=jnp.asarray(sel, BF16), wuv_pad=b16(wuv_pad), wga=b16(wga),
                bga=b_gla_a.reshape(1, -1), g_q=g_q.reshape(1, -1), g_kv=g_kv.reshape(1, -1),
                g_gla=g_gla.reshape(1, -1), wbr=b16(w_branch), wo=b16(w_o))


TOKEN_TILE = 512
MIX_TILE = 256


def _trunk_layer(x, mod, ng, ffw, layer, lw, tabs, mixer_fn):
    n, t, _ = x.shape
    x, h = _ffn_call(x, mod, ng, *ffw, layer, 0, 0, True, tm=TOKEN_TILE)
    pr = _proj_calls(h, lw, tabs, tm=TOKEN_TILE)
    oa, ob, oc, st = mixer_fn(pr)
    x = _mix_call(h, x, mod, ng, oa, ob, oc, pr['gg'], lw, tm=MIX_TILE)
    x = _ffn_call(x, mod, ng, *ffw, layer, 1, 2, False, tm=TOKEN_TILE)
    kv_shape = (n, t, DSA_KV_HEADS, DSA_HEAD_DIM)
    return x, (pr['ckv'], pr['krope'], pr['kb'].reshape(kv_shape), pr['vb'].reshape(kv_shape), pr['ki'], st)


def kernel(x_prompt, x_sample, cache_mla_latent, cache_mla_krope, cache_dsa_k, cache_dsa_v, cache_dsa_idx_k,
           state_gla, page_table, c_prompt, c_sample, w_mod, b_mod, norm_g, w_ff_gate, w_ff_up, w_ff_down,
           w_in, g_q, g_kv, w_uq, w_ukv, w_gla_a, b_gla_a, g_gla, w_branch, w_o):
    depth = w_in.shape[0]
    nb, seq, _ = x_prompt.shape
    ndec, tdec, _ = x_sample.shape
    n_phys = cache_dsa_k.shape[1]
    past = page_table.shape[1] * PAGE_SIZE

    consts = _placement_consts()
    mod_all = _mod_call(jnp.concatenate([c_prompt, c_sample], axis=0), w_mod.astype(BF16), b_mod)
    ffw = (w_ff_gate.astype(BF16), w_ff_up.astype(BF16), w_ff_down.astype(BF16))
    tabs_p = _rope_tables(jnp.arange(seq), 1)
    tabs_s = _rope_tables(past + jnp.arange(tdec), max(1, TOKEN_TILE // tdec))
    kvd = DSA_KV_HEADS * DSA_HEAD_DIM
    krt_pool = jnp.transpose(cache_mla_krope, (0, 1, 3, 2))
    ikt_pool = jnp.transpose(cache_dsa_idx_k, (0, 1, 3, 2))
    kt_pool = jnp.transpose(cache_dsa_k, (0, 1, 3, 4, 2)).reshape(depth, n_phys, kvd, PAGE_SIZE)
    vt_pool = jnp.transpose(cache_dsa_v, (0, 1, 3, 4, 2)).reshape(depth, n_phys, kvd, PAGE_SIZE)
    zero_state = jnp.zeros((nb, GLA_HEADS, GLA_DK, GLA_DV), F32)

    xp, xs = x_prompt, x_sample
    new_p, new_s = [], []
    for l in range(depth):
        lw = _prep_layer(w_in[l], g_q[l], g_kv[l], w_uq[l], w_ukv[l], w_gla_a[l], b_gla_a[l], g_gla[l],
                         w_branch[l], w_o[l])
        mod_p = mod_all[l, :nb].reshape(nb, 9, D_MODEL)
        mod_s = mod_all[l, nb:].reshape(ndec, 9, D_MODEL)

        def prompt_mixer(pp):
            oa = _mla_prompt_call(pp['qcat'], pp['kcat'], lw['wuv_pad'])
            ob = _dsa_prompt_call(pp, consts)
            oc, st = _gla_call(pp['qc'], pp['kc'], pp['vc'], pp['la'], zero_state, consts['ones'])
            return oa, ob, oc, st

        def sample_mixer(ps):
            oa, scores = _smp1_call(l, ps, page_table, cache_mla_latent, krt_pool, ikt_pool, consts, lw['wuv_pad'])
            ob = _smp2_call(l, ps, scores, page_table, kt_pool, vt_pool, consts)
            oc, st = _gla_call(ps['qc'], ps['kc'], ps['vc'], ps['la'], state_gla[l], consts['ones'])
            return oa, ob, oc, st

        xp, st_p = _trunk_layer(xp, mod_p, norm_g[l], ffw, l, lw, tabs_p, prompt_mixer)
        xs, st_s = _trunk_layer(xs, mod_s, norm_g[l], ffw, l, lw, tabs_s, sample_mixer)
        new_p.append(st_p)
        new_s.append(st_s)

    lat_p, krope_p, k_p, v_p, ik_p, gla_p = [jnp.stack(a) for a in zip(*new_p)]
    lat_s, krope_s, k_s, v_s, ik_s, gla_s = [jnp.stack(a) for a in zip(*new_s)]
    return (xp, xs, lat_p, lat_s, krope_p, krope_s, k_p, k_s, v_p, v_s, ik_p, ik_s, gla_p, gla_s)
```

```python
import functools

import numpy as np
import jax
import jax.numpy as jnp
from jax import lax
from jax.experimental import pallas as pl
from jax.experimental.pallas import tpu as pltpu

F32 = jnp.float32
BF16 = jnp.bfloat16

D_MODEL = 1024
PAGE_SIZE = 128
MLA_HEADS = 8
MLA_Q_LORA = 256
MLA_KV_LORA = 128
MLA_NOPE = 64
MLA_ROPE = 32
MLA_V = 64
MLA_SCALE = (MLA_NOPE + MLA_ROPE) ** -0.5
DSA_HEADS = 8
DSA_KV_HEADS = 2
DSA_HEAD_DIM = 64
DSA_SCALE = DSA_HEAD_DIM ** -0.5
IDX_HEADS = 4
IDX_DIM = 64
IDX_SCALE = (IDX_HEADS * IDX_DIM) ** -0.5
IDX_TOPK_MAX = 256
GLA_HEADS = 4
GLA_DK = 64
GLA_DV = 128
GLA_GATE_RANK = 16
GLA_TAU = 16.0
GLA_CHUNK = 16
D_FF = 2816
N_BRANCH = 3
BRANCH_W = 512
FFN_RESID = 0.5
ROPE_THETA = 10000.0
EPS = 1e-6

LANES = 128
SUBLANES = 8
VMEM_LIMIT = 56 * 1024 * 1024
INT_MIN = -2 ** 31

_IN_SPLITS = (
    ('mla_cq', 256), ('mla_ckv', 128), ('mla_krope', 32), ('dsa_q', 512), ('dsa_k', 128), ('dsa_v', 128),
    ('idx_q', 256), ('idx_k', 64), ('idx_w', 4), ('gla_q', 256), ('gla_k', 256), ('gla_v', 512),
    ('gla_a', 16), ('gla_g', 512), ('merge', 3072),
)
_IN_OFF = {}
_o = 0
for _nm, _w in _IN_SPLITS:
    _IN_OFF[_nm] = (_o, _o + _w)
    _o += _w


def _cparams(*sem):
    return pltpu.CompilerParams(dimension_semantics=sem, vmem_limit_bytes=VMEM_LIMIT)


def _dot(a, b):
    return jnp.dot(a, b, preferred_element_type=F32)


def _dot_nt(a, b):
    return lax.dot_general(a, b, (((1,), (1,)), ((), ())), preferred_element_type=F32)


def _sigmoid(x):
    return 1.0 / (1.0 + jnp.exp(-x))


def _silu(x):
    return x * _sigmoid(x)


def _log_sigmoid(x):
    return -(jnp.maximum(-x, 0.0) + jnp.log(1.0 + jnp.exp(-jnp.abs(x))))


def _rms(x, g):
    return x * lax.rsqrt(jnp.mean(x * x, axis=-1, keepdims=True) + EPS) * g


def _rope(x, cos, sin_signed, half):
    w = x.shape[-1]
    lane = lax.broadcasted_iota(jnp.int32, x.shape, 1)
    first = (lane % (2 * half)) < half
    rot = jnp.where(first, pltpu.roll(x, w - half, 1), pltpu.roll(x, half, 1))
    return x * cos + rot * sin_signed


def _tok_tiling(n, t, tm):
    if t >= tm:
        assert t % tm == 0
        return 1, tm, t // tm
    assert tm % t == 0 and n % (tm // t) == 0
    return tm // t, t, 1


def _tok_spec(sb, tt, tps, width):
    return pl.BlockSpec((sb, tt, width), lambda i, *_: (i // tps, i % tps, 0))


def _seq_spec(sb, tps, rows, width):
    return pl.BlockSpec((sb, rows, width), lambda i, *_: (i // tps, 0, 0))


def _tab_spec(sb, tt, tps, width):
    return pl.BlockSpec((sb * tt, width), lambda i, *_: (i % tps, 0))


def _const_spec(shape):
    nd = len(shape)
    return pl.BlockSpec(shape, lambda *_: (0,) * nd)


def _mod_kernel(c_ref, w_ref, b_ref, o_ref):
    a = _silu(c_ref[...]).astype(BF16)
    o_ref[0] = _dot(a, w_ref[0]) + b_ref[0]


def _mod_call(c_all, w_mod16, b_mod):
    depth = w_mod16.shape[0]
    nc = c_all.shape[0]
    ncol = w_mod16.shape[2]
    tn = 1024
    return pl.pallas_call(
        _mod_kernel,
        grid=(depth, ncol // tn),
        in_specs=[pl.BlockSpec((nc, D_MODEL), lambda l, j: (0, 0)),
                  pl.BlockSpec((1, D_MODEL, tn), lambda l, j: (l, 0, j)),
                  pl.BlockSpec((1, 1, tn), lambda l, j: (l, 0, j))],
        out_specs=pl.BlockSpec((1, nc, tn), lambda l, j: (l, 0, j)),
        out_shape=jax.ShapeDtypeStruct((depth, nc, ncol), F32),
        compiler_params=_cparams("arbitrary", "arbitrary"),
        name="adaln_mod",
    )(c_all, w_mod16, b_mod.reshape(depth, 1, ncol))


def _ffn_kernel(x_ref, mod_ref, g_ref, wg_ref, wu_ref, wd_ref, *rest, sub, emit_h):
    if emit_h:
        o_ref, h_ref, hn_ref, acc_ref = rest
    else:
        o_ref, hn_ref, acc_ref = rest
    k = pl.program_id(1)
    sb, tt, d = x_ref.shape

    @pl.when(k == 0)
    def _():
        x = x_ref[...]
        shift = mod_ref[:, 3 * sub:3 * sub + 1, :]
        scale = mod_ref[:, 3 * sub + 1:3 * sub + 2, :]
        h = _rms(x, g_ref[2 * sub:2 * sub + 1, :]) * (1.0 + scale) + shift
        hn_ref[...] = h.reshape(sb * tt, d).astype(BF16)
        acc_ref[...] = jnp.zeros_like(acc_ref)

    hn = hn_ref[...]
    gte = _dot(hn, wg_ref[...])
    up = _dot(hn, wu_ref[...])
    act = (_silu(gte) * up).astype(BF16)
    acc_ref[...] += _dot(act, wd_ref[...])

    @pl.when(k == pl.num_programs(1) - 1)
    def _():
        y = acc_ref[...].reshape(sb, tt, d)
        gate = mod_ref[:, 3 * sub + 2:3 * sub + 3, :]
        out = x_ref[...] + FFN_RESID * (gate * _rms(y, g_ref[2 * sub + 1:2 * sub + 2, :]))
        o_ref[...] = out
        if emit_h:
            hm = _rms(out, g_ref[2:3, :]) * (1.0 + mod_ref[:, 4:5, :]) + mod_ref[:, 3:4, :]
            h_ref[...] = hm.astype(BF16)


def _ffn_call(x, mod, norm_g, wg16, wu16, wd16, layer, which, sub, emit_h, tm=512, ck=1408):
    n, t, d = x.shape
    sb, tt, tps = _tok_tiling(n, t, tm)
    grid = ((n // sb) * tps, D_FF // ck)
    out_shape = [jax.ShapeDtypeStruct((n, t, d), F32)]
    out_specs = [_tok_spec(sb, tt, tps, d)]
    if emit_h:
        out_shape.append(jax.ShapeDtypeStruct((n, t, d), BF16))
        out_specs.append(_tok_spec(sb, tt, tps, d))
    res = pl.pallas_call(
        functools.partial(_ffn_kernel, sub=sub, emit_h=emit_h),
        grid=grid,
        in_specs=[_tok_spec(sb, tt, tps, d),
                  _seq_spec(sb, tps, 9, d),
                  pl.BlockSpec((6, d), lambda i, k: (0, 0)),
                  pl.BlockSpec((None, None, d, ck), lambda i, k: (layer, which, 0, k)),
                  pl.BlockSpec((None, None, d, ck), lambda i, k: (layer, which, 0, k)),
                  pl.BlockSpec((None, None, ck, d), lambda i, k: (layer, which, k, 0))],
        out_specs=out_specs,
        out_shape=out_shape,
        scratch_shapes=[pltpu.VMEM((sb * tt, d), BF16), pltpu.VMEM((sb * tt, d), F32)],
        compiler_params=_cparams("parallel", "arbitrary"),
        name="ffn",
    )(x, mod, norm_g, wg16, wu16, wd16)
    return res if emit_h else res[0]


def _proj_mla_kernel(h_ref, wa_ref, gq_ref, gkv_ref, wuq_ref, wabs_ref, sel_ref,
                     cq_ref, sq_ref, ck_ref, sk_ref, qcat_ref, kcat_ref, ckv_ref, krope_ref):
    sb, tt, d = h_ref.shape
    tm = sb * tt
    z = _dot(h_ref[...].reshape(tm, d), wa_ref[...])
    cq = _rms(z[:, :MLA_Q_LORA], gq_ref[...])
    qh = _dot(cq.astype(BF16), wuq_ref[...])
    qn = (qh[:, :512] * MLA_SCALE).astype(BF16)
    qr = (_rope(qh[:, 512:], cq_ref[...], sq_ref[...], MLA_ROPE // 2) * MLA_SCALE).astype(BF16)
    qcat = _dot(qn, wabs_ref[...]) + _dot(qr, sel_ref[...])
    qcat_ref[...] = qcat.astype(BF16).reshape(sb, tt, qcat.shape[-1])
    ckv = _rms(z[:, 256:384], gkv_ref[...])
    kr = _rope(z[:, 384:512], ck_ref[...], sk_ref[...], MLA_ROPE // 2)
    ckv_ref[...] = ckv.reshape(sb, tt, MLA_KV_LORA)
    krope_ref[...] = kr[:, :MLA_ROPE].reshape(sb, tt, MLA_ROPE)
    kcat_ref[...] = jnp.concatenate([ckv, kr], axis=1).astype(BF16).reshape(sb, tt, 256)


def _proj_dsa_kernel(h_ref, wb_ref, c_ref, s_ref, ckw_ref, skw_ref,
                     qb_ref, kb_ref, vb_ref, qi_ref, kw_ref, ki_ref):
    sb, tt, d = h_ref.shape
    tm = sb * tt
    z = _dot(h_ref[...].reshape(tm, d), wb_ref[...])
    c1, s1 = c_ref[...], s_ref[...]
    c2, s2 = jnp.concatenate([c1, c1], axis=1), jnp.concatenate([s1, s1], axis=1)
    c4, s4 = jnp.concatenate([c2, c2], axis=1), jnp.concatenate([s2, s2], axis=1)
    half = DSA_HEAD_DIM // 2
    qb = _rope(z[:, :512], c4, s4, half) * DSA_SCALE
    qb_ref[...] = qb.astype(BF16).reshape(sb, tt, 512)
    kb_ref[...] = _rope(z[:, 512:640], c1, s1, half).reshape(sb, tt, 128)
    vb_ref[...] = z[:, 640:768].reshape(sb, tt, 128)
    qi_ref[...] = _rope(z[:, 768:1024], c2, s2, half).reshape(sb, tt, 256)
    kw = _rope(z[:, 1024:1152], ckw_ref[...], skw_ref[...], half)
    kw_ref[...] = kw.reshape(sb, tt, 128)
    ki_ref[...] = kw[:, :IDX_DIM].reshape(sb, tt, IDX_DIM)


def _proj_gla_kernel(h_ref, wc_ref, wga_ref, bga_ref, qc_ref, kc_ref, vc_ref, la_ref, gg_ref):
    sb, tt, d = h_ref.shape
    tm = sb * tt
    z = _dot(h_ref[...].reshape(tm, d), wc_ref[...])
    qc_ref[...] = (z[:, :256] * (GLA_DK ** -0.5)).reshape(sb, tt, 256)
    kc_ref[...] = z[:, 256:512].reshape(sb, tt, 256)
    vc_ref[...] = z[:, 512:1024].reshape(sb, tt, 512)
    gg_ref[...] = z[:, 1024:1536].reshape(sb, tt, 512)
    logit = _dot(z[:, 1536:1664].astype(BF16), wga_ref[...]) + bga_ref[...]
    la_ref[...] = (_log_sigmoid(logit) / GLA_TAU).reshape(sb, tt, 256)


def _proj_calls(h, lw, tabs, tm=512):
    n, t, d = h.shape
    sb, tt, tps = _tok_tiling(n, t, tm)
    grid = ((n // sb) * tps,)
    tok = lambda w: _tok_spec(sb, tt, tps, w)
    tab = lambda w: _tab_spec(sb, tt, tps, w)
    shp = lambda w, dt=F32: jax.ShapeDtypeStruct((n, t, w), dt)
    cp = _cparams("parallel")

    qcat, kcat, ckv, krope = pl.pallas_call(
        _proj_mla_kernel, grid=grid,
        in_specs=[tok(d), _const_spec((d, 512)), _const_spec((1, 256)), _const_spec((1, 128)),
                  _const_spec((256, 768)), _const_spec((512, 2048)), _const_spec((256, 2048)),
                  tab(256), tab(256), tab(128), tab(128)],
        out_specs=[tok(2048), tok(256), tok(128), tok(32)],
        out_shape=[shp(2048, BF16), shp(256, BF16), shp(128), shp(32)],
        compiler_params=cp, name="proj_mla",
    )(h, lw['wa'], lw['g_q'], lw['g_kv'], lw['w_uq'], lw['wabs'], lw['sel'],
      tabs['cos32q'], tabs['sin32q'], tabs['cos32k'], tabs['sin32k'])

    qb, kb, vb, qi, kw, ki = pl.pallas_call(
        _proj_dsa_kernel, grid=grid,
        in_specs=[tok(d), _const_spec((d, 1152)), tab(128), tab(128), tab(128), tab(128)],
        out_specs=[tok(512), tok(128), tok(128), tok(256), tok(128), tok(64)],
        out_shape=[shp(512, BF16), shp(128), shp(128), shp(256), shp(128), shp(64)],
        compiler_params=cp, name="proj_dsa",
    )(h, lw['wb'], tabs['cos64'], tabs['sin64'], tabs['cos64kw'], tabs['sin64kw'])

    qc, kc, vc, la, gg = pl.pallas_call(
        _proj_gla_kernel, grid=grid,
        in_specs=[tok(d), _const_spec((d, 1664)), _const_spec((128, 256)), _const_spec((1, 256))],
        out_specs=[tok(256), tok(256), tok(512), tok(256), tok(512)],
        out_shape=[shp(256), shp(256), shp(512), shp(256), shp(512)],
        compiler_params=cp, name="proj_gla",
    )(h, lw['wc'], lw['wga'], lw['bga'])
    return dict(qcat=qcat, kcat=kcat, ckv=ckv, krope=krope, qb=qb, kb=kb, vb=vb, qi=qi, kw=kw, ki=ki,
                qc=qc, kc=kc, vc=vc, la=la, gg=gg)


def _flash_update(s, v1, m_ref, acc_ref, g, guard):
    m_prev = m_ref[g]
    m_new = jnp.maximum(m_prev, jnp.max(s, axis=1, keepdims=True))
    m_use = jnp.where(m_new == -jnp.inf, 0.0, m_new) if guard else m_new
    alpha = jnp.exp(m_prev - m_use)
    p = jnp.exp(s - jnp.concatenate([m_use] * (s.shape[1] // LANES), axis=1))
    acc_ref[g] = jnp.concatenate([alpha, alpha], axis=1) * acc_ref[g] + _dot(p.astype(BF16), v1)
    m_ref[g] = m_new


def _mla_p_kernel(q_ref, k_ref, wuv_ref, o_ref, m_ref, acc_ref, *, tq, tk, ngrp):
    qi = pl.program_id(1)
    nh = MLA_HEADS
    hpg = nh // ngrp
    qs = [jnp.concatenate([q_ref[0, :, 256 * h:256 * (h + 1)] for h in range(g * hpg, (g + 1) * hpg)], axis=0)
          for g in range(ngrp)]
    m_ref[...] = jnp.full_like(m_ref, -jnp.inf)
    acc_ref[...] = jnp.zeros_like(acc_ref)
    ones = jnp.ones((tk, LANES), BF16)

    def step(j, mask):
        start = pl.multiple_of(j * tk, tk)
        k = k_ref[0, pl.ds(start, tk), :]
        v1 = jnp.concatenate([k[:, :MLA_KV_LORA], ones], axis=1)
        for g in range(ngrp):
            s = _dot_nt(qs[g], k)
            if mask is not None:
                s = jnp.where(mask, s, -jnp.inf)
            _flash_update(s, v1, m_ref, acc_ref, g, False)

    nfull = qi * (tq // tk)

    def body(j, c):
        step(j, None)
        return c

    lax.fori_loop(0, nfull, body, 0)
    row = lax.broadcasted_iota(jnp.int32, (tq, tk), 0)
    col = lax.broadcasted_iota(jnp.int32, (tq, tk), 1)
    for dj in range(tq // tk):
        mk = row >= col + dj * tk
        step(nfull + dj, jnp.concatenate([mk] * hpg, axis=0))

    out = jnp.zeros((tq, 512), F32)
    for g in range(ngrp):
        acc = acc_ref[g]
        o_lat = (acc[:, :MLA_KV_LORA] / acc[:, MLA_KV_LORA:]).astype(BF16)
        for hh in range(hpg):
            out = out + _dot(o_lat[hh * tq:(hh + 1) * tq], wuv_ref[g * hpg + hh])
    o_ref[0] = out.astype(BF16)


def _mla_prompt_call(qcat, kcat, wuv_pad, tq=256, tk=256, ngrp=2):
    n, t, _ = qcat.shape
    rows = MLA_HEADS // ngrp * tq
    return pl.pallas_call(
        functools.partial(_mla_p_kernel, tq=tq, tk=tk, ngrp=ngrp),
        grid=(n, t // tq),
        in_specs=[pl.BlockSpec((1, tq, 2048), lambda b, i: (b, i, 0)),
                  pl.BlockSpec((1, t, 256), lambda b, i: (b, 0, 0)),
                  _const_spec((MLA_HEADS, 128, 512))],
        out_specs=pl.BlockSpec((1, tq, 512), lambda b, i: (b, i, 0)),
        out_shape=jax.ShapeDtypeStruct((n, t, 512), BF16),
        scratch_shapes=[pltpu.VMEM((ngrp, rows, LANES), F32), pltpu.VMEM((ngrp, rows, 2 * LANES), F32)],
        compiler_params=_cparams("parallel", "arbitrary"),
        name="mla_prompt",
    )(qcat, kcat, wuv_pad)


def _sort_key(score):
    bits = lax.bitcast_convert_type(score + 0.0, jnp.int32)
    return jnp.where(bits < 0, bits ^ jnp.int32(0x7FFFFFFF), bits)


def _lane_fold(x):
    parts = [x[:, i * LANES:(i + 1) * LANES] for i in range(x.shape[1] // LANES)]
    while len(parts) > 1:
        parts = [parts[i] + parts[i + 1] if i + 1 < len(parts) else parts[i] for i in range(0, len(parts), 2)]
    return parts[0]


def _count(mask):
    return jnp.sum(_lane_fold(jnp.where(mask, 1.0, 0.0)), axis=1, keepdims=True)


def _sublane_fold(x):
    parts = [x[i * SUBLANES:(i + 1) * SUBLANES] for i in range(x.shape[0] // SUBLANES)]
    while len(parts) > 1:
        parts = [parts[i] + parts[i + 1] if i + 1 < len(parts) else parts[i] for i in range(0, len(parts), 2)]
    return parts[0]


def _kth_and_ties(count_fn, vshape, kk, idx_bits):
    kkf = jnp.float32(kk)
    thr = jnp.where(count_fn(lambda k, i: k >= 0) >= kkf, jnp.int32(0), jnp.int32(INT_MIN))

    def vbody(b, thr):
        cand = thr | lax.shift_left(jnp.int32(1), jnp.int32(30) - b)
        return jnp.where(count_fn(lambda k, i: k >= cand) >= kkf, cand, thr)

    thr = lax.fori_loop(0, 31, vbody, thr)
    need = kkf - count_fn(lambda k, i: k > thr)
    surplus = jnp.max(count_fn(lambda k, i: k == thr) - need)

    def tie(_):
        def pbody(b, p):
            cand = p | lax.shift_left(jnp.int32(1), jnp.int32(idx_bits - 1) - b)
            return jnp.where(count_fn(lambda k, i: (k == thr) & (i < cand)) <= need, cand, p)
        return lax.fori_loop(0, idx_bits, pbody, jnp.zeros(vshape, jnp.int32))

    p = lax.cond(surplus > 0.0, tie, lambda _: jnp.full(vshape, 2 ** idx_bits - 1, jnp.int32), 0)
    return thr, p


def _selected(key, pos, thr, tie_end):
    return (key > thr) | ((key == thr) & (pos < tie_end))


def _split3_q(qf, ph_ref, plo_ref):
    q_hi = qf.astype(BF16)
    q_lo = (qf - q_hi.astype(F32)).astype(BF16)
    return jnp.concatenate(
        [_dot(q_hi, ph_ref[h]) + _dot(q_lo, plo_ref[h]) for h in range(IDX_HEADS)], axis=0).astype(BF16)


def _split3_k(kk, pkh_ref, pkl_ref):
    w = kk.shape[1]
    k_hi = kk.astype(BF16)
    k_lo = (kk - k_hi.astype(F32)).astype(BF16)
    return (_dot(k_hi, pkh_ref[:w, :]) + _dot(k_lo, pkl_ref[:w, :])).astype(BF16)


def _idx_weighted(dots, wq, tq):
    score = jnp.zeros((tq, dots.shape[1]), F32)
    for h in range(IDX_HEADS):
        wh = wq[:, IDX_DIM + h:IDX_DIM + h + 1] * IDX_SCALE
        score = score + wh * jnp.maximum(dots[h * tq:(h + 1) * tq], 0.0)
    return score


def _idx_scores(q3, k3, wq, tq):
    return _idx_weighted(_dot_nt(q3, k3), wq, tq)


def _dsa_p_kernel(qi_ref, kwq_ref, kwk_ref, qb_ref, kb_ref, vb_ref, ph_ref, plo_ref, pkh_ref, pkl_ref,
                  pq_ref, po_ref, o_ref, k3_ref, kb16_ref, vb1_ref, key_ref, keyt_ref, m_ref, acc_ref,
                  *, tq, ck, topk):
    it = pl.program_id(1)
    s_len = kwk_ref.shape[1]
    grp = DSA_HEADS // DSA_KV_HEADS

    @pl.when(it == 0)
    def _():
        k3_ref[...] = _split3_k(kwk_ref[0], pkh_ref, pkl_ref)
        kb16_ref[...] = kb_ref[0].astype(BF16)
        vb1_ref[...] = jnp.concatenate([vb_ref[0], jnp.ones((s_len, LANES), F32)], axis=1).astype(BF16)

    nc = lax.shift_right_logical(it * tq + (tq + ck - 1), int(ck).bit_length() - 1)
    qpos = it * tq + lax.broadcasted_iota(jnp.int32, (tq, ck), 0)
    lanes = lax.broadcasted_iota(jnp.int32, (tq, ck), 1)
    chunk_start = lambda c: pl.multiple_of(c * ck, ck)

    q3 = _split3_q(qi_ref[0], ph_ref, plo_ref)
    wq = kwq_ref[0]

    def sc_body(c, carry):
        start = chunk_start(c)
        score = _idx_scores(q3, k3_ref[pl.ds(start, ck), :], wq, tq)
        key = _sort_key(jnp.where(start + lanes <= qpos, score, -jnp.inf))
        key_ref[:, pl.ds(start, ck)] = key
        keyt_ref[pl.ds(start, ck), :] = key.T
        return carry

    lax.fori_loop(0, nc, sc_body, 0)

    kpos_t = lax.broadcasted_iota(jnp.int32, (ck, tq), 0)

    def count_fn(pred):
        def body(c, acc):
            start = chunk_start(c)
            hit = pred(keyt_ref[pl.ds(start, ck), :], start + kpos_t)
            return acc + _sublane_fold(jnp.where(hit, 1.0, 0.0))
        acc = lax.fori_loop(0, nc, body, jnp.zeros((SUBLANES, tq), F32))
        return jnp.sum(acc, axis=0, keepdims=True)

    idx_bits = int(s_len).bit_length()

    def search(_):
        thr_row, tie_row = _kth_and_ties(count_fn, (1, tq), topk, idx_bits)
        return (jnp.broadcast_to(thr_row, (LANES, tq)).T, jnp.broadcast_to(tie_row, (LANES, tq)).T)

    thr, tie_end = lax.cond(
        it * tq + tq > topk, search,
        lambda _: (jnp.full((tq, LANES), INT_MIN, jnp.int32), jnp.full((tq, LANES), 2 ** idx_bits - 1, jnp.int32)),
        0)
    nrep = ck // LANES
    thr = jnp.concatenate([thr] * nrep, axis=1)
    tie_end = jnp.concatenate([tie_end] * nrep, axis=1)

    m_ref[...] = jnp.full_like(m_ref, -jnp.inf)
    acc_ref[...] = jnp.zeros_like(acc_ref)
    qb = qb_ref[0]
    qrows = []
    for j in range(DSA_KV_HEADS):
        qp = _dot(qb, pq_ref[j])
        qrows.append(jnp.concatenate([qp[:, 128 * g:128 * (g + 1)] for g in range(grp)], axis=0).astype(BF16))

    def at_body(c, carry):
        start = chunk_start(c)
        kc = key_ref[:, pl.ds(start, ck)]
        pos = start + lanes
        sel = _selected(kc, pos, thr, tie_end) & (pos <= qpos)
        sel4 = jnp.concatenate([sel] * grp, axis=0)
        kb = kb16_ref[pl.ds(start, ck), :]
        v1 = vb1_ref[pl.ds(start, ck), :]
        for j in range(DSA_KV_HEADS):
            s = jnp.where(sel4, _dot_nt(qrows[j], kb), -jnp.inf)
            _flash_update(s, v1, m_ref, acc_ref, j, True)
        return carry

    lax.fori_loop(0, nc, at_body, 0)
    out = jnp.zeros((tq, 512), F32)
    for j in range(DSA_KV_HEADS):
        acc = acc_ref[j]
        o = (acc[:, :LANES] / acc[:, LANES:]).astype(BF16)
        for g in range(grp):
            out = out + _dot(o[g * tq:(g + 1) * tq], po_ref[j * grp + g])
    o_ref[0] = out.astype(BF16)


def _dsa_prompt_call(pr, consts, tq=256, ck=512):
    n, t, _ = pr['qi'].shape
    ck = min(ck, t)
    tq = min(tq, t)
    assert t % ck == 0 and ck & (ck - 1) == 0
    topk = min(IDX_TOPK_MAX, t // 4)
    rows = DSA_HEADS // DSA_KV_HEADS * tq
    qspec = lambda w: pl.BlockSpec((1, tq, w), lambda b, i: (b, i, 0))
    sspec = lambda w: pl.BlockSpec((1, t, w), lambda b, i: (b, 0, 0))
    return pl.pallas_call(
        functools.partial(_dsa_p_kernel, tq=tq, ck=ck, topk=topk),
        grid=(n, t // tq),
        in_specs=[qspec(256), qspec(128), sspec(128), qspec(512), sspec(128), sspec(128),
                  _const_spec((4, 256, 256)), _const_spec((4, 256, 256)), _const_spec((128, 256)),
                  _const_spec((128, 256)), _const_spec((2, 512, 512)), _const_spec((8, 128, 512))],
        out_specs=qspec(512),
        out_shape=jax.ShapeDtypeStruct((n, t, 512), BF16),
        scratch_shapes=[pltpu.VMEM((t, 256), BF16), pltpu.VMEM((t, 128), BF16), pltpu.VMEM((t, 256), BF16),
                        pltpu.VMEM((tq, t), jnp.int32), pltpu.VMEM((t, tq), jnp.int32),
                        pltpu.VMEM((DSA_KV_HEADS, rows, LANES), F32), pltpu.VMEM((DSA_KV_HEADS, rows, 2 * LANES), F32)],
        compiler_params=_cparams("parallel", "arbitrary"),
        name="dsa_prompt",
    )(pr['qi'], pr['kw'], pr['kw'], pr['qb'], pr['kb'], pr['vb'],
      consts['ph'], consts['pl'], consts['pkh'], consts['pkl'], consts['pq'], consts['po'])


def _gla_kernel(q_ref, k_ref, v_ref, la_ref, s0_ref, ones_ref, o_ref, sfin_ref, st_ref, *, chunk, tt):
    ti = pl.program_id(1)
    nh, dk, dv = GLA_HEADS, GLA_DK, GLA_DV
    hd = nh * dk

    @pl.when(ti == 0)
    def _():
        st_ref[...] = s0_ref[0].astype(F32).reshape(hd, dv)

    q = q_ref[0]
    k = k_ref[0]
    v = v_ref[0]
    la = la_ref[0]
    rc = lax.broadcasted_iota(jnp.int32, (tt, hd), 0) % chunk
    rcv = lax.broadcasted_iota(jnp.int32, (tt, nh * dv), 0) % chunk

    b = la
    sft = 1
    while sft < chunk:
        b = b + jnp.where(rc >= sft, pltpu.roll(b, sft, 0), 0.0)
        sft *= 2
    bl = jnp.where(rc == chunk - 1, b, 0.0)
    sft = 1
    while sft < chunk:
        bl = bl + jnp.where(rc < chunk - sft, pltpu.roll(bl, tt - sft, 0), 0.0)
        sft *= 2

    ones = ones_ref[...]
    o_intra = jnp.zeros((tt, nh * dv), F32)
    for lag in range(chunk):
        if lag == 0:
            pr = q * k
            vs = v
        else:
            valid = rc >= lag
            ks = pltpu.roll(k, lag, 0)
            bs = pltpu.roll(b, lag, 0)
            pr = jnp.where(valid, q * ks * jnp.exp(jnp.minimum(b - bs, 0.0)), 0.0)
            vs = jnp.where(rcv >= lag, pltpu.roll(v, lag, 0), 0.0)
        att = _dot(pr.astype(BF16), ones)
        o_intra = o_intra + att * vs

    qe = q * jnp.exp(b)
    kd = k * jnp.exp(bl - b)
    pad = LANES - tt
    if pad:
        kd = jnp.concatenate([kd, jnp.zeros((pad, hd), F32)], axis=0)
        vp = jnp.concatenate([v, jnp.zeros((pad, nh * dv), F32)], axis=0)
        blp = jnp.concatenate([bl, jnp.zeros((pad, hd), F32)], axis=0)
    else:
        vp, blp = v, bl
    kdt = kd.T.astype(BF16)
    dect = jnp.exp(blp.T)
    v16 = vp.astype(BF16)
    lane = lax.broadcasted_iota(jnp.int32, (hd, LANES), 1)
    rowh = lax.broadcasted_iota(jnp.int32, (nh * chunk, hd), 0) // chunk
    colh = lax.broadcasted_iota(jnp.int32, (nh * chunk, hd), 1) // dk
    headmask = rowh == colh
    st = st_ref[...]
    outs = []
    for c in range(tt // chunk):
        qe_c = qe[c * chunk:(c + 1) * chunk]
        x = jnp.where(headmask, jnp.concatenate([qe_c] * nh, axis=0), 0.0).astype(BF16)
        r = _dot(x, st.astype(BF16))
        outs.append(jnp.concatenate([r[h * chunk:(h + 1) * chunk] for h in range(nh)], axis=1))
        inchunk = (lane >= c * chunk) & (lane < (c + 1) * chunk)
        u = _dot(jnp.where(inchunk, kdt, jnp.zeros_like(kdt)), v16)
        upd = jnp.concatenate([u[h * dk:(h + 1) * dk, h * dv:(h + 1) * dv] for h in range(nh)], axis=0)
        last = (c + 1) * chunk - 1
        st = dect[:, last:last + 1] * st + upd
    st_ref[...] = st
    o_inter = outs[0] if len(outs) == 1 else jnp.concatenate(outs, axis=0)
    o_ref[0] = o_inter + o_intra

    @pl.when(ti == pl.num_programs(1) - 1)
    def _():
        sfin_ref[0] = st.reshape(nh, dk, dv)


def _gla_call(qc, kc, vc, la, s0, ones_blk):
    n, t, _ = qc.shape
    chunk = int(np.gcd(t, GLA_CHUNK))
    tt = min(t, LANES)
    assert t % tt == 0 and tt % chunk == 0
    tspec = lambda w: pl.BlockSpec((1, tt, w), lambda b, i: (b, i, 0))
    sspec = pl.BlockSpec((1, GLA_HEADS, GLA_DK, GLA_DV), lambda b, i: (b, 0, 0, 0))
    return pl.pallas_call(
        functools.partial(_gla_kernel, chunk=chunk, tt=tt),
        grid=(n, t // tt),
        in_specs=[tspec(256), tspec(256), tspec(512), tspec(256), sspec, _const_spec((256, 512))],
        out_specs=[tspec(512), sspec],
        out_shape=[jax.ShapeDtypeStruct((n, t, 512), F32),
                   jax.ShapeDtypeStruct((n, GLA_HEADS, GLA_DK, GLA_DV), F32)],
        scratch_shapes=[pltpu.VMEM((GLA_HEADS * GLA_DK, GLA_DV), F32)],
        compiler_params=_cparams("parallel", "arbitrary"),
        name="gla",
    )(qc, kc, vc, la, s0, ones_blk)


def _mix_kernel(h_ref, x_ref, mod_ref, g_ref, oa_ref, ob_ref, oc_ref, gg_ref, ggla_ref,
                wm_ref, wbr_ref, wo_ref, o_ref):
    sb, tt, d = x_ref.shape
    tm = sb * tt
    h = h_ref[...].reshape(tm, d)
    gates = _sigmoid(_dot(h, wm_ref[...]))
    oc = oc_ref[...].reshape(tm, BRANCH_W)
    gg = gg_ref[...].reshape(tm, BRANCH_W)
    ggla = ggla_ref[...]
    parts = []
    for hh in range(GLA_HEADS):
        seg = oc[:, GLA_DV * hh:GLA_DV * (hh + 1)]
        parts.append(_rms(seg, ggla))
    ocn = (jnp.concatenate(parts, axis=1) * _silu(gg)).astype(BF16)
    ya = _dot(oa_ref[...].reshape(tm, BRANCH_W), wbr_ref[0])
    yb = _dot(ob_ref[...].reshape(tm, BRANCH_W), wbr_ref[1])
    yc = _dot(ocn, wbr_ref[2])
    y = gates[:, :d] * ya + gates[:, d:2 * d] * yb + gates[:, 2 * d:] * yc
    out = _dot(y.astype(BF16), wo_ref[...]).reshape(sb, tt, d)
    o_ref[...] = x_ref[...] + mod_ref[:, 5:6, :] * _rms(out, g_ref[3:4, :])


def _mix_call(h, x, mod, norm_g, oa, ob, oc, gg, lw, tm=256):
    n, t, d = x.shape
    sb, tt, tps = _tok_tiling(n, t, tm)
    tok = lambda w: _tok_spec(sb, tt, tps, w)
    return pl.pallas_call(
        _mix_kernel,
        grid=((n // sb) * tps,),
        in_specs=[tok(d), tok(d), _seq_spec(sb, tps, 9, d), _const_spec((6, d)),
                  tok(512), tok(512), tok(512), tok(512), _const_spec((1, 128)),
                  _const_spec((d, 3 * d)), _const_spec((3, 512, d)), _const_spec((d, d))],
        out_specs=tok(d),
        out_shape=jax.ShapeDtypeStruct((n, t, d), F32),
        compiler_params=_cparams("parallel"),
        name="mix_out",
    )(h, x, mod, norm_g, oa, ob, oc, gg, lw['g_gla'], lw['wm'], lw['wbr'], lw['wo'])


def _smp1_kernel(pt_ref, q_ref, qi_ref, kwq_ref, kcn_ref, kwn_ref, ph_ref, plo_ref, pkh_ref, pkl_ref, wuv_ref,
                 *rest, pps, t_new):
    lat = rest[:pps]
    kro = rest[pps:2 * pps]
    ikp = rest[2 * pps:3 * pps]
    o_ref, sc_ref, m_ref, l_ref, acc_ref = rest[3 * pps:]
    j = pl.program_id(1)
    nj = pl.num_programs(1)
    nh = MLA_HEADS
    tq = t_new

    @pl.when(j == 0)
    def _():
        m_ref[...] = jnp.full_like(m_ref, -jnp.inf)
        l_ref[...] = jnp.zeros_like(l_ref)
        acc_ref[...] = jnp.zeros_like(acc_ref)

    qall = q_ref[0].astype(F32)
    qa = jnp.concatenate([qall[:, 256 * h:256 * h + 128] for h in range(nh)], axis=0).astype(BF16)
    qr = jnp.concatenate([qall[:, 256 * h + 128:256 * h + 256] for h in range(nh)], axis=0).astype(BF16)
    q3 = _split3_q(qi_ref[0], ph_ref, plo_ref)
    wq = kwq_ref[0]

    def idx_scores(kslab):
        return _idx_scores(q3, _split3_k(kslab, pkh_ref, pkl_ref), wq, tq)

    def flash(s, vals):
        m_prev = m_ref[...]
        m_new = jnp.maximum(m_prev, jnp.max(s, axis=1, keepdims=True))
        alpha = jnp.exp(m_prev - m_new)
        p = jnp.exp(s - m_new)
        l_ref[...] = alpha * l_ref[...] + jnp.sum(p, axis=1, keepdims=True)
        acc_ref[...] = alpha * acc_ref[...] + _dot(p.astype(BF16), vals)
        m_ref[...] = m_new

    @pl.when(j < nj - 1)
    def _():
        c = jnp.concatenate([lat[i][...] for i in range(pps)], axis=0).astype(BF16)
        rt = jnp.concatenate([kro[i][...] for i in range(pps)], axis=1).astype(BF16)
        flash(_dot_nt(qa, c) + _dot(qr[:, :MLA_ROPE], rt), c)
        kt = jnp.concatenate([ikp[i][...] for i in range(pps)], axis=1)
        k_hi = kt.astype(BF16)
        k_lo = (kt - k_hi.astype(F32)).astype(BF16)
        k3t = jnp.concatenate([k_hi, k_lo, k_hi, jnp.zeros_like(k_hi)], axis=0)
        sc_ref[0] = _idx_weighted(_dot(q3, k3t), wq, tq)

    @pl.when(j == nj - 1)
    def _():
        kc = jnp.concatenate([kcn_ref[0].astype(F32), jnp.zeros((LANES - tq, 256), F32)], axis=0).astype(BF16)
        s = _dot_nt(jnp.concatenate([qa, qr], axis=1), kc)
        row = lax.broadcasted_iota(jnp.int32, (nh * tq, LANES), 0) % tq
        col = lax.broadcasted_iota(jnp.int32, (nh * tq, LANES), 1)
        flash(jnp.where(col <= row, s, -jnp.inf), kc[:, :MLA_KV_LORA])
        o_lat = acc_ref[...] / l_ref[...]
        out = jnp.zeros((tq, 512), F32)
        for h in range(nh):
            out = out + _dot(o_lat[h * tq:(h + 1) * tq].astype(BF16), wuv_ref[h])
        o_ref[0] = out.astype(BF16)
        kslab = jnp.concatenate([kwn_ref[0], jnp.zeros((LANES - tq, LANES), F32)], axis=0)
        sc = idx_scores(kslab)
        r8 = lax.broadcasted_iota(jnp.int32, (tq, LANES), 0)
        c8 = lax.broadcasted_iota(jnp.int32, (tq, LANES), 1)
        tail = jnp.where(c8 <= r8, sc, -jnp.inf)
        fill = jnp.full((tq, pps * PAGE_SIZE - LANES), -jnp.inf, F32)
        sc_ref[0] = jnp.concatenate([tail, fill], axis=1)


def _page_spec(layer, i, pps, n_pages, rows, cols):
    return pl.BlockSpec((None, None, rows, cols),
                        lambda b, j, pt: (layer, pt[b, jnp.minimum(j * pps + i, n_pages - pps + i)], 0, 0))


def _smp1_call(layer, pr, page_table, lat_pool, krt_pool, ikt_pool, consts, wuv_pad, pps=16):
    n, t, _ = pr['qcat'].shape
    n_pages = page_table.shape[1]
    pps = min(pps, n_pages)
    assert n_pages % pps == 0
    nj = n_pages // pps + 1
    width = n_pages * PAGE_SIZE + LANES
    page_spec = lambda i, rows, cols: _page_spec(layer, i, pps, n_pages, rows, cols)

    nspec = lambda w: pl.BlockSpec((1, t, w), lambda b, j, pt: (b, 0, 0))
    grid_spec = pltpu.PrefetchScalarGridSpec(
        num_scalar_prefetch=1,
        grid=(n, nj),
        in_specs=[nspec(2048), nspec(256), nspec(128), nspec(256), nspec(128),
                  _const_spec((4, 256, 256)), _const_spec((4, 256, 256)), _const_spec((128, 256)),
                  _const_spec((128, 256)), _const_spec((MLA_HEADS, 128, 512))]
                 + [page_spec(i, PAGE_SIZE, MLA_KV_LORA) for i in range(pps)]
                 + [page_spec(i, MLA_ROPE, PAGE_SIZE) for i in range(pps)]
                 + [page_spec(i, IDX_DIM, PAGE_SIZE) for i in range(pps)],
        out_specs=[pl.BlockSpec((1, t, 512), lambda b, j, pt: (b, 0, 0)),
                   pl.BlockSpec((1, t, pps * PAGE_SIZE), lambda b, j, pt: (b, 0, j))],
        scratch_shapes=[pltpu.VMEM((MLA_HEADS * t, 1), F32), pltpu.VMEM((MLA_HEADS * t, 1), F32),
                        pltpu.VMEM((MLA_HEADS * t, MLA_KV_LORA), F32)],
    )
    return pl.pallas_call(
        functools.partial(_smp1_kernel, pps=pps, t_new=t),
        grid_spec=grid_spec,
        out_shape=[jax.ShapeDtypeStruct((n, t, 512), BF16), jax.ShapeDtypeStruct((n, t, width), F32)],
        compiler_params=_cparams("parallel", "arbitrary"),
        name="mla_idx_sample",
    )(page_table, pr['qcat'], pr['qi'], pr['kw'], pr['kcat'], pr['kw'],
      consts['ph'], consts['pl'], consts['pkh'], consts['pkl'], wuv_pad,
      *([lat_pool] * pps), *([krt_pool] * pps), *([ikt_pool] * pps))


def _smp_topk_kernel(sc_ref, thr_ref, tie_ref, *, topk):
    sb, tq, width = sc_ref.shape
    rows = sb * tq
    key = _sort_key(sc_ref[...].reshape(rows, width))
    idx = lax.broadcasted_iota(jnp.int32, (rows, width), 1)
    thr, tie_end = _kth_and_ties(lambda pred: _count(pred(key, idx)), (rows, 1), topk, int(width).bit_length())
    thr_ref[...] = jnp.broadcast_to(thr, (rows, LANES)).reshape(sb, tq, LANES)
    tie_ref[...] = jnp.broadcast_to(tie_end, (rows, LANES)).reshape(sb, tq, LANES)


def _smp_topk_call(scores, topk, sb=8):
    n, t, width = scores.shape
    sb = min(sb, n)
    assert n % sb == 0
    spec = lambda w: pl.BlockSpec((sb, t, w), lambda i: (i, 0, 0))
    return pl.pallas_call(
        functools.partial(_smp_topk_kernel, topk=topk),
        grid=(n // sb,),
        in_specs=[spec(width)],
        out_specs=[spec(LANES), spec(LANES)],
        out_shape=[jax.ShapeDtypeStruct((n, t, LANES), jnp.int32)] * 2,
        compiler_params=_cparams("parallel"),
        name="topk_sample",
    )(scores)


def _smp2_kernel(pt_ref, sc_ref, thr_ref, tie_ref, qb_ref, kbn_ref, vbn_ref, pq_ref, po_ref, *rest, pps, t_new):
    kp = rest[:pps]
    vp = rest[pps:2 * pps]
    o_ref, q_scr, m_ref, l_ref, acc_ref = rest[2 * pps:]
    j = pl.program_id(1)
    nj = pl.num_programs(1)
    tq = t_new
    grp = DSA_HEADS // DSA_KV_HEADS
    cw = pps * PAGE_SIZE

    @pl.when(j == 0)
    def _():
        m_ref[...] = jnp.full_like(m_ref, -jnp.inf)
        l_ref[...] = jnp.zeros_like(l_ref)
        acc_ref[...] = jnp.zeros_like(acc_ref)
        qb = qb_ref[0]
        rows = []
        for jj in range(DSA_KV_HEADS):
            qp = _dot(qb, pq_ref[jj])
            rows += [qp[:, 128 * g:128 * (g + 1)] for g in range(grp)]
        q_scr[...] = jnp.concatenate(rows, axis=0).astype(BF16)

    def flash(s, pv):
        m_prev = m_ref[...]
        m_new = jnp.maximum(m_prev, jnp.max(s, axis=1, keepdims=True))
        m_safe = jnp.where(m_new == -jnp.inf, 0.0, m_new)
        alpha = jnp.exp(m_prev - m_safe)
        p = jnp.exp(s - m_safe)
        l_ref[...] = alpha * l_ref[...] + jnp.sum(p, axis=1, keepdims=True)
        acc_ref[...] = alpha * acc_ref[...] + pv(p.astype(BF16))
        m_ref[...] = m_new

    q = q_scr[...]
    sc = sc_ref[0]
    pos = j * cw + lax.broadcasted_iota(jnp.int32, (tq, cw), 1)
    thr = jnp.concatenate([thr_ref[0]] * pps, axis=1)
    tie_end = jnp.concatenate([tie_ref[0]] * pps, axis=1)
    sel = _selected(_sort_key(sc), pos, thr, tie_end) & (sc > -jnp.inf)
    msk = jnp.concatenate([sel] * (DSA_KV_HEADS * grp), axis=0)

    @pl.when(j < nj - 1)
    def _():
        kt = jnp.concatenate([kp[i][...] for i in range(pps)], axis=1).astype(BF16)
        vt = jnp.concatenate([vp[i][...] for i in range(pps)], axis=1).astype(BF16)
        flash(jnp.where(msk, _dot(q, kt), -jnp.inf), lambda p: _dot_nt(p, vt))

    @pl.when(j == nj - 1)
    def _():
        zr = jnp.zeros((LANES - tq, LANES), F32)
        k = jnp.concatenate([kbn_ref[0], zr], axis=0).astype(BF16)
        v = jnp.concatenate([vbn_ref[0], zr], axis=0).astype(BF16)
        flash(jnp.where(msk[:, :LANES], _dot_nt(q, k), -jnp.inf), lambda p: _dot(p, v))
        o = acc_ref[...] / l_ref[...]
        out = jnp.zeros((tq, 512), F32)
        for jj in range(DSA_KV_HEADS):
            for g in range(grp):
                r0 = (jj * grp + g) * tq
                out = out + _dot(o[r0:r0 + tq].astype(BF16), po_ref[jj * grp + g])
        o_ref[0] = out.astype(BF16)


def _smp2_call(layer, pr, scores, page_table, kt_pool, vt_pool, consts, pps=16):
    n, t, _ = pr['qb'].shape
    n_pages = page_table.shape[1]
    pps = min(pps, n_pages)
    nj = n_pages // pps + 1
    width = scores.shape[2]
    assert width == n_pages * PAGE_SIZE + LANES
    past = n_pages * PAGE_SIZE
    topk = min(IDX_TOPK_MAX, (past + t) // 4)
    thr, tie_end = _smp_topk_call(scores, topk)
    nrow = DSA_HEADS * t
    cw = pps * PAGE_SIZE
    page_spec = lambda i: _page_spec(layer, i, pps, n_pages, DSA_KV_HEADS * DSA_HEAD_DIM, PAGE_SIZE)

    nspec = lambda w: pl.BlockSpec((1, t, w), lambda b, j, pt: (b, 0, 0))
    grid_spec = pltpu.PrefetchScalarGridSpec(
        num_scalar_prefetch=1,
        grid=(n, nj),
        in_specs=[pl.BlockSpec((1, t, cw), lambda b, j, pt: (b, 0, j)), nspec(LANES), nspec(LANES),
                  nspec(512), nspec(128), nspec(128),
                  _const_spec((2, 512, 512)), _const_spec((8, 128, 512))]
                 + [page_spec(i) for i in range(pps)] + [page_spec(i) for i in range(pps)],
        out_specs=pl.BlockSpec((1, t, 512), lambda b, j, pt: (b, 0, 0)),
        scratch_shapes=[pltpu.VMEM((nrow, 128), BF16),
                        pltpu.VMEM((nrow, 1), F32), pltpu.VMEM((nrow, 1), F32), pltpu.VMEM((nrow, 128), F32)],
    )
    return pl.pallas_call(
        functools.partial(_smp2_kernel, pps=pps, t_new=t),
        grid_spec=grid_spec,
        out_shape=jax.ShapeDtypeStruct((n, t, 512), BF16),
        compiler_params=_cparams("parallel", "arbitrary"),
        name="dsa_sample",
    )(page_table, scores, thr, tie_end, pr['qb'], pr['kb'], pr['vb'], consts['pq'], consts['po'],
      *([kt_pool] * pps), *([vt_pool] * pps))


def _placement_consts():
    ph = np.zeros((4, 256, 256), np.float32)
    plo = np.zeros((4, 256, 256), np.float32)
    for h in range(4):
        for dd in range(64):
            ph[h, 64 * h + dd, dd] = 1.0
            ph[h, 64 * h + dd, 64 + dd] = 1.0
            plo[h, 64 * h + dd, 128 + dd] = 1.0
    pkh = np.zeros((128, 256), np.float32)
    pkl = np.zeros((128, 256), np.float32)
    for dd in range(64):
        pkh[dd, dd] = 1.0
        pkh[dd, 128 + dd] = 1.0
        pkl[dd, 64 + dd] = 1.0
    pq = np.zeros((2, 512, 512), np.float32)
    po = np.zeros((8, 128, 512), np.float32)
    for j in range(2):
        for g in range(4):
            head = 4 * j + g
            for e in range(64):
                pq[j, 64 * head + e, 128 * g + 64 * j + e] = 1.0
                po[head, 64 * j + e, 64 * head + e] = 1.0
    ones = np.zeros((256, 512), np.float32)
    for h in range(4):
        ones[64 * h:64 * (h + 1), 128 * h:128 * (h + 1)] = 1.0
    mk = lambda a: jnp.asarray(a, BF16)
    return dict(ph=mk(ph), pl=mk(plo), pkh=mk(pkh), pkl=mk(pkl), pq=mk(pq), po=mk(po), ones=mk(ones))


def _rope_tables(pos, rep):
    def base(half):
        inv = ROPE_THETA ** (-jnp.arange(half, dtype=F32) / half)
        ang = pos.astype(F32)[:, None] * inv[None, :]
        c, s = jnp.cos(ang), jnp.sin(ang)
        return jnp.concatenate([c, c], axis=1), jnp.concatenate([-s, s], axis=1)

    c32, s32 = base(16)
    c64, s64 = base(32)
    tpos = pos.shape[0]
    one = lambda w: jnp.ones((tpos, w), F32)
    zero = lambda w: jnp.zeros((tpos, w), F32)
    tabs = dict(
        cos32q=jnp.tile(c32, (1, 8)), sin32q=jnp.tile(s32, (1, 8)),
        cos32k=jnp.concatenate([c32, one(96)], axis=1), sin32k=jnp.concatenate([s32, zero(96)], axis=1),
        cos64=jnp.tile(c64, (1, 2)), sin64=jnp.tile(s64, (1, 2)),
        cos64kw=jnp.concatenate([c64, one(64)], axis=1), sin64kw=jnp.concatenate([s64, zero(64)], axis=1),
    )
    if rep > 1:
        tabs = {k: jnp.tile(v, (rep, 1)) for k, v in tabs.items()}
    return tabs


def _prep_layer(w_in, g_q, g_kv, w_uq, w_ukv, w_gla_a, b_gla_a, g_gla, w_branch, w_o):
    col = lambda nm: w_in[:, _IN_OFF[nm][0]:_IN_OFF[nm][1]]
    zc = lambda w: jnp.zeros((D_MODEL, w), w_in.dtype)
    wa = jnp.concatenate([col('mla_cq'), col('mla_ckv'), col('mla_krope'), zc(96)], axis=1)
    wb = jnp.concatenate([col('dsa_q'), col('dsa_k'), col('dsa_v'), col('idx_q'), col('idx_k'), col('idx_w'),
                          zc(60)], axis=1)
    wc = jnp.concatenate([col('gla_q'), col('gla_k'), col('gla_v'), col('gla_g'), col('gla_a'), zc(112)], axis=1)
    wm = col('merge')
    uq = w_uq.reshape(MLA_Q_LORA, MLA_HEADS, MLA_NOPE + MLA_ROPE)
    w_uq_r = jnp.concatenate([uq[:, :, :MLA_NOPE].reshape(MLA_Q_LORA, -1),
                              uq[:, :, MLA_NOPE:].reshape(MLA_Q_LORA, -1)], axis=1)
    w_uk = w_ukv[:, :, :MLA_NOPE]
    w_uv = w_ukv[:, :, MLA_NOPE:]
    wabs = jnp.zeros((MLA_HEADS * MLA_NOPE, MLA_HEADS * 256), w_in.dtype)
    wuv_pad = jnp.zeros((MLA_HEADS, MLA_KV_LORA, 512), w_in.dtype)
    sel = np.zeros((MLA_HEADS * MLA_ROPE, MLA_HEADS * 256), np.float32)
    for h in range(MLA_HEADS):
        wabs = wabs.at[MLA_NOPE * h:MLA_NOPE * (h + 1), 256 * h:256 * h + MLA_KV_LORA].set(w_uk[:, h, :].T)
        wuv_pad = wuv_pad.at[h, :, MLA_V * h:MLA_V * (h + 1)].set(w_uv[:, h, :])
        for dd in range(MLA_ROPE):
            sel[MLA_ROPE * h + dd, 256 * h + MLA_KV_LORA + dd] = 1.0
    wga = jnp.concatenate([w_gla_a, jnp.zeros((LANES - GLA_GATE_RANK, w_gla_a.shape[1]), w_gla_a.dtype)], axis=0)
    b16 = lambda a: a.astype(BF16)
    return dict(wa=b16(wa), wb=b16(wb), wc=b16(wc), wm=b16(wm), w_uq=b16(w_uq_r), wabs=b16(wabs),
                sel=jnp.asarray(sel, BF16), wuv_pad=b16(wuv_pad), wga=b16(wga),
                bga=b_gla_a.reshape(1, -1), g_q=g_q.reshape(1, -1), g_kv=g_kv.reshape(1, -1),
                g_gla=g_gla.reshape(1, -1), wbr=b16(w_branch), wo=b16(w_o))


TOKEN_TILE = 512
MIX_TILE = 256


def _trunk_layer(x, mod, ng, ffw, layer, lw, tabs, mixer_fn):
    n, t, _ = x.shape
    x, h = _ffn_call(x, mod, ng, *ffw, layer, 0, 0, True, tm=TOKEN_TILE)
    pr = _proj_calls(h, lw, tabs, tm=TOKEN_TILE)
    oa, ob, oc, st = mixer_fn(pr)
    x = _mix_call(h, x, mod, ng, oa, ob, oc, pr['gg'], lw, tm=MIX_TILE)
    x = _ffn_call(x, mod, ng, *ffw, layer, 1, 2, False, tm=TOKEN_TILE)
    kv_shape = (n, t, DSA_KV_HEADS, DSA_HEAD_DIM)
    return x, (pr['ckv'], pr['krope'], pr['kb'].reshape(kv_shape), pr['vb'].reshape(kv_shape), pr['ki'], st)


def kernel(x_prompt, x_sample, cache_mla_latent, cache_mla_krope, cache_dsa_k, cache_dsa_v, cache_dsa_idx_k,
           state_gla, page_table, c_prompt, c_sample, w_mod, b_mod, norm_g, w_ff_gate, w_ff_up, w_ff_down,
           w_in, g_q, g_kv, w_uq, w_ukv, w_gla_a, b_gla_a, g_gla, w_branch, w_o):
    depth = w_in.shape[0]
    nb, seq, _ = x_prompt.shape
    ndec, tdec, _ = x_sample.shape
    n_phys = cache_dsa_k.shape[1]
    past = page_table.shape[1] * PAGE_SIZE

    consts = _placement_consts()
    mod_all = _mod_call(jnp.concatenate([c_prompt, c_sample], axis=0), w_mod.astype(BF16), b_mod)
    ffw = (w_ff_gate.astype(BF16), w_ff_up.astype(BF16), w_ff_down.astype(BF16))
    tabs_p = _rope_tables(jnp.arange(seq), 1)
    tabs_s = _rope_tables(past + jnp.arange(tdec), max(1, TOKEN_TILE // tdec))
    kvd = DSA_KV_HEADS * DSA_HEAD_DIM
    krt_pool = jnp.transpose(cache_mla_krope, (0, 1, 3, 2))
    ikt_pool = jnp.transpose(cache_dsa_idx_k, (0, 1, 3, 2))
    kt_pool = jnp.transpose(cache_dsa_k, (0, 1, 3, 4, 2)).reshape(depth, n_phys, kvd, PAGE_SIZE)
    vt_pool = jnp.transpose(cache_dsa_v, (0, 1, 3, 4, 2)).reshape(depth, n_phys, kvd, PAGE_SIZE)
    zero_state = jnp.zeros((nb, GLA_HEADS, GLA_DK, GLA_DV), F32)

    xp, xs = x_prompt, x_sample
    new_p, new_s = [], []
    for l in range(depth):
        lw = _prep_layer(w_in[l], g_q[l], g_kv[l], w_uq[l], w_ukv[l], w_gla_a[l], b_gla_a[l], g_gla[l],
                         w_branch[l], w_o[l])
        mod_p = mod_all[l, :nb].reshape(nb, 9, D_MODEL)
        mod_s = mod_all[l, nb:].reshape(ndec, 9, D_MODEL)

        def prompt_mixer(pp):
            oa = _mla_prompt_call(pp['qcat'], pp['kcat'], lw['wuv_pad'])
            ob = _dsa_prompt_call(pp, consts)
            oc, st = _gla_call(pp['qc'], pp['kc'], pp['vc'], pp['la'], zero_state, consts['ones'])
            return oa, ob, oc, st

        def sample_mixer(ps):
            oa, scores = _smp1_call(l, ps, page_table, cache_mla_latent, krt_pool, ikt_pool, consts, lw['wuv_pad'])
            ob = _smp2_call(l, ps, scores, page_table, kt_pool, vt_pool, consts)
            oc, st = _gla_call(ps['qc'], ps['kc'], ps['vc'], ps['la'], state_gla[l], consts['ones'])
            return oa, ob, oc, st

        xp, st_p = _trunk_layer(xp, mod_p, norm_g[l], ffw, l, lw, tabs_p, prompt_mixer)
        xs, st_s = _trunk_layer(xs, mod_s, norm_g[l], ffw, l, lw, tabs_s, sample_mixer)
        new_p.append(st_p)
        new_s.append(st_s)

    lat_p, krope_p, k_p, v_p, ik_p, gla_p = [jnp.stack(a) for a in zip(*new_p)]
    lat_s, krope_s, k_s, v_s, ik_s, gla_s = [jnp.stack(a) for a in zip(*new_s)]
    return (xp, xs, lat_p, lat_s, krope_p, krope_s, k_p, k_s, v_p, v_s, ik_p, ik_s, gla_p, gla_s)
```

```python
import functools

import numpy as np
import jax
import jax.numpy as jnp
from jax import lax
from jax.experimental import pallas as pl
from jax.experimental.pallas import tpu as pltpu

F32 = jnp.float32
BF16 = jnp.bfloat16

D_MODEL = 1024
PAGE_SIZE = 128
MLA_HEADS = 8
MLA_Q_LORA = 256
MLA_KV_LORA = 128
MLA_NOPE = 64
MLA_ROPE = 32
MLA_V = 64
MLA_SCALE = (MLA_NOPE + MLA_ROPE) ** -0.5
DSA_HEADS = 8
DSA_KV_HEADS = 2
DSA_HEAD_DIM = 64
DSA_SCALE = DSA_HEAD_DIM ** -0.5
IDX_HEADS = 4
IDX_DIM = 64
IDX_SCALE = (IDX_HEADS * IDX_DIM) ** -0.5
IDX_TOPK_MAX = 256
GLA_HEADS = 4
GLA_DK = 64
GLA_DV = 128
GLA_GATE_RANK = 16
GLA_TAU = 16.0
GLA_CHUNK = 16
D_FF = 2816
N_BRANCH = 3
BRANCH_W = 512
FFN_RESID = 0.5
ROPE_THETA = 10000.0
EPS = 1e-6

LANES = 128
SUBLANES = 8
VMEM_LIMIT = 56 * 1024 * 1024
INT_MIN = -2 ** 31
INT_MAX = 2 ** 31 - 1
MASKED = -1e30

_IN_SPLITS = (
    ('mla_cq', 256), ('mla_ckv', 128), ('mla_krope', 32), ('dsa_q', 512), ('dsa_k', 128), ('dsa_v', 128),
    ('idx_q', 256), ('idx_k', 64), ('idx_w', 4), ('gla_q', 256), ('gla_k', 256), ('gla_v', 512),
    ('gla_a', 16), ('gla_g', 512), ('merge', 3072),
)
_IN_OFF = {}
_o = 0
for _nm, _w in _IN_SPLITS:
    _IN_OFF[_nm] = (_o, _o + _w)
    _o += _w


def _cparams(*sem):
    return pltpu.CompilerParams(dimension_semantics=sem, vmem_limit_bytes=VMEM_LIMIT)


def _dot(a, b):
    return jnp.dot(a, b, preferred_element_type=F32)


def _dot_nt(a, b):
    return lax.dot_general(a, b, (((1,), (1,)), ((), ())), preferred_element_type=F32)


def _sigmoid(x):
    return 1.0 / (1.0 + jnp.exp(-x))


def _silu(x):
    return x * _sigmoid(x)


def _log_sigmoid(x):
    return -(jnp.maximum(-x, 0.0) + jnp.log(1.0 + jnp.exp(-jnp.abs(x))))


def _rms(x, g):
    return x * lax.rsqrt(jnp.mean(x * x, axis=-1, keepdims=True) + EPS) * g


def _rope(x, cos, sin_signed, half):
    w = x.shape[-1]
    lane = lax.broadcasted_iota(jnp.int32, x.shape, 1)
    first = (lane % (2 * half)) < half
    rot = jnp.where(first, pltpu.roll(x, w - half, 1), pltpu.roll(x, half, 1))
    return x * cos + rot * sin_signed


def _tok_tiling(n, t, tm):
    if t >= tm:
        assert t % tm == 0
        return 1, tm, t // tm
    assert tm % t == 0 and n % (tm // t) == 0
    return tm // t, t, 1


def _tok_spec(sb, tt, tps, width):
    return pl.BlockSpec((sb, tt, width), lambda i, *_: (i // tps, i % tps, 0))


def _seq_spec(sb, tps, rows, width):
    return pl.BlockSpec((sb, rows, width), lambda i, *_: (i // tps, 0, 0))


def _tab_spec(sb, tt, tps, width):
    return pl.BlockSpec((sb * tt, width), lambda i, *_: (i % tps, 0))


def _const_spec(shape):
    nd = len(shape)
    return pl.BlockSpec(shape, lambda *_: (0,) * nd)


def _mod_kernel(c_ref, w_ref, b_ref, o_ref):
    a = _silu(c_ref[...]).astype(BF16)
    o_ref[0] = _dot(a, w_ref[0]) + b_ref[0]


def _mod_call(c_all, w_mod16, b_mod):
    depth = w_mod16.shape[0]
    nc = c_all.shape[0]
    ncol = w_mod16.shape[2]
    tn = 1024
    return pl.pallas_call(
        _mod_kernel,
        grid=(depth, ncol // tn),
        in_specs=[pl.BlockSpec((nc, D_MODEL), lambda l, j: (0, 0)),
                  pl.BlockSpec((1, D_MODEL, tn), lambda l, j: (l, 0, j)),
                  pl.BlockSpec((1, 1, tn), lambda l, j: (l, 0, j))],
        out_specs=pl.BlockSpec((1, nc, tn), lambda l, j: (l, 0, j)),
        out_shape=jax.ShapeDtypeStruct((depth, nc, ncol), F32),
        compiler_params=_cparams("arbitrary", "arbitrary"),
        name="adaln_mod",
    )(c_all, w_mod16, b_mod.reshape(depth, 1, ncol))


def _ffn_kernel(x_ref, mod_ref, g_ref, wg_ref, wu_ref, wd_ref, *rest, sub, emit_h):
    if emit_h:
        o_ref, h_ref, hn_ref, acc_ref = rest
    else:
        o_ref, hn_ref, acc_ref = rest
    k = pl.program_id(1)
    sb, tt, d = x_ref.shape

    @pl.when(k == 0)
    def _():
        x = x_ref[...]
        shift = mod_ref[:, 3 * sub:3 * sub + 1, :]
        scale = mod_ref[:, 3 * sub + 1:3 * sub + 2, :]
        h = _rms(x, g_ref[2 * sub:2 * sub + 1, :]) * (1.0 + scale) + shift
        hn_ref[...] = h.reshape(sb * tt, d).astype(BF16)
        acc_ref[...] = jnp.zeros_like(acc_ref)

    hn = hn_ref[...]
    gte = _dot(hn, wg_ref[...])
    up = _dot(hn, wu_ref[...])
    act = (_silu(gte) * up).astype(BF16)
    acc_ref[...] += _dot(act, wd_ref[...])

    @pl.when(k == pl.num_programs(1) - 1)
    def _():
        y = acc_ref[...].reshape(sb, tt, d)
        gate = mod_ref[:, 3 * sub + 2:3 * sub + 3, :]
        out = x_ref[...] + FFN_RESID * (gate * _rms(y, g_ref[2 * sub + 1:2 * sub + 2, :]))
        o_ref[...] = out
        if emit_h:
            hm = _rms(out, g_ref[2:3, :]) * (1.0 + mod_ref[:, 4:5, :]) + mod_ref[:, 3:4, :]
            h_ref[...] = hm.astype(BF16)


def _ffn_call(x, mod, norm_g, wg16, wu16, wd16, layer, which, sub, emit_h, tm=512, ck=D_FF):
    n, t, d = x.shape
    sb, tt, tps = _tok_tiling(n, t, tm)
    grid = ((n // sb) * tps, D_FF // ck)
    wmode = dict(pipeline_mode=pl.Buffered(1)) if ck == D_FF else {}
    out_shape = [jax.ShapeDtypeStruct((n, t, d), F32)]
    out_specs = [_tok_spec(sb, tt, tps, d)]
    if emit_h:
        out_shape.append(jax.ShapeDtypeStruct((n, t, d), BF16))
        out_specs.append(_tok_spec(sb, tt, tps, d))
    res = pl.pallas_call(
        functools.partial(_ffn_kernel, sub=sub, emit_h=emit_h),
        grid=grid,
        in_specs=[_tok_spec(sb, tt, tps, d),
                  _seq_spec(sb, tps, 9, d),
                  pl.BlockSpec((6, d), lambda i, k: (0, 0)),
                  pl.BlockSpec((None, None, d, ck), lambda i, k: (layer, which, 0, k), **wmode),
                  pl.BlockSpec((None, None, d, ck), lambda i, k: (layer, which, 0, k), **wmode),
                  pl.BlockSpec((None, None, ck, d), lambda i, k: (layer, which, k, 0), **wmode)],
        out_specs=out_specs,
        out_shape=out_shape,
        scratch_shapes=[pltpu.VMEM((sb * tt, d), BF16), pltpu.VMEM((sb * tt, d), F32)],
        compiler_params=_cparams("parallel", "arbitrary"),
        name="ffn",
    )(x, mod, norm_g, wg16, wu16, wd16)
    return res if emit_h else res[0]


def _proj_mla_kernel(h_ref, wa_ref, gq_ref, gkv_ref, wuq_ref, wabs_ref, sel_ref,
                     cq_ref, sq_ref, ck_ref, sk_ref, qcat_ref, kcat_ref, ckv_ref, krope_ref):
    sb, tt, d = h_ref.shape
    tm = sb * tt
    z = _dot(h_ref[...].reshape(tm, d), wa_ref[...])
    cq = _rms(z[:, :MLA_Q_LORA], gq_ref[...])
    qh = _dot(cq.astype(BF16), wuq_ref[...])
    qn = (qh[:, :512] * MLA_SCALE).astype(BF16)
    qr = (_rope(qh[:, 512:], cq_ref[...], sq_ref[...], MLA_ROPE // 2) * MLA_SCALE).astype(BF16)
    qcat = _dot(qn, wabs_ref[...]) + _dot(qr, sel_ref[...])
    qcat_ref[...] = qcat.astype(BF16).reshape(sb, tt, qcat.shape[-1])
    ckv = _rms(z[:, 256:384], gkv_ref[...])
    kr = _rope(z[:, 384:512], ck_ref[...], sk_ref[...], MLA_ROPE // 2)
    ckv_ref[...] = ckv.reshape(sb, tt, MLA_KV_LORA)
    krope_ref[...] = kr[:, :MLA_ROPE].reshape(sb, tt, MLA_ROPE)
    kcat_ref[...] = jnp.concatenate([ckv, kr], axis=1).astype(BF16).reshape(sb, tt, 256)


def _proj_dsa_kernel(h_ref, wb_ref, c_ref, s_ref, ckw_ref, skw_ref,
                     qb_ref, kb_ref, vb_ref, qi_ref, kw_ref, ki_ref):
    sb, tt, d = h_ref.shape
    tm = sb * tt
    z = _dot(h_ref[...].reshape(tm, d), wb_ref[...])
    c1, s1 = c_ref[...], s_ref[...]
    c2, s2 = jnp.concatenate([c1, c1], axis=1), jnp.concatenate([s1, s1], axis=1)
    c4, s4 = jnp.concatenate([c2, c2], axis=1), jnp.concatenate([s2, s2], axis=1)
    half = DSA_HEAD_DIM // 2
    qb = _rope(z[:, :512], c4, s4, half) * DSA_SCALE
    qb_ref[...] = qb.astype(BF16).reshape(sb, tt, 512)
    kb_ref[...] = _rope(z[:, 512:640], c1, s1, half).reshape(sb, tt, 128)
    vb_ref[...] = z[:, 640:768].reshape(sb, tt, 128)
    qi_ref[...] = _rope(z[:, 768:1024], c2, s2, half).reshape(sb, tt, 256)
    kw = _rope(z[:, 1024:1152], ckw_ref[...], skw_ref[...], half)
    kw_ref[...] = kw.reshape(sb, tt, 128)
    ki_ref[...] = kw[:, :IDX_DIM].reshape(sb, tt, IDX_DIM)


def _proj_gla_kernel(h_ref, wc_ref, wga_ref, bga_ref, qc_ref, kc_ref, vc_ref, la_ref, gg_ref):
    sb, tt, d = h_ref.shape
    tm = sb * tt
    z = _dot(h_ref[...].reshape(tm, d), wc_ref[...])
    qc_ref[...] = (z[:, :256] * (GLA_DK ** -0.5)).reshape(sb, tt, 256)
    kc_ref[...] = z[:, 256:512].reshape(sb, tt, 256)
    vc_ref[...] = z[:, 512:1024].reshape(sb, tt, 512)
    gg_ref[...] = z[:, 1024:1536].reshape(sb, tt, 512)
    logit = _dot(z[:, 1536:1664].astype(BF16), wga_ref[...]) + bga_ref[...]
    la_ref[...] = (_log_sigmoid(logit) / GLA_TAU).reshape(sb, tt, 256)


def _proj_calls(h, lw, tabs, tm=512):
    n, t, d = h.shape
    sb, tt, tps = _tok_tiling(n, t, tm)
    grid = ((n // sb) * tps,)
    tok = lambda w: _tok_spec(sb, tt, tps, w)
    tab = lambda w: _tab_spec(sb, tt, tps, w)
    shp = lambda w, dt=F32: jax.ShapeDtypeStruct((n, t, w), dt)
    cp = _cparams("parallel")

    qcat, kcat, ckv, krope = pl.pallas_call(
        _proj_mla_kernel, grid=grid,
        in_specs=[tok(d), _const_spec((d, 512)), _const_spec((1, 256)), _const_spec((1, 128)),
                  _const_spec((256, 768)), _const_spec((512, 2048)), _const_spec((256, 2048)),
                  tab(256), tab(256), tab(128), tab(128)],
        out_specs=[tok(2048), tok(256), tok(128), tok(32)],
        out_shape=[shp(2048, BF16), shp(256, BF16), shp(128), shp(32)],
        compiler_params=cp, name="proj_mla",
    )(h, lw['wa'], lw['g_q'], lw['g_kv'], lw['w_uq'], lw['wabs'], lw['sel'],
      tabs['cos32q'], tabs['sin32q'], tabs['cos32k'], tabs['sin32k'])

    qb, kb, vb, qi, kw, ki = pl.pallas_call(
        _proj_dsa_kernel, grid=grid,
        in_specs=[tok(d), _const_spec((d, 1152)), tab(128), tab(128), tab(128), tab(128)],
        out_specs=[tok(512), tok(128), tok(128), tok(256), tok(128), tok(64)],
        out_shape=[shp(512, BF16), shp(128), shp(128), shp(256), shp(128), shp(64)],
        compiler_params=cp, name="proj_dsa",
    )(h, lw['wb'], tabs['cos64'], tabs['sin64'], tabs['cos64kw'], tabs['sin64kw'])

    qc, kc, vc, la, gg = pl.pallas_call(
        _proj_gla_kernel, grid=grid,
        in_specs=[tok(d), _const_spec((d, 1664)), _const_spec((128, 256)), _const_spec((1, 256))],
        out_specs=[tok(256), tok(256), tok(512), tok(256), tok(512)],
        out_shape=[shp(256), shp(256), shp(512), shp(256), shp(512)],
        compiler_params=cp, name="proj_gla",
    )(h, lw['wc'], lw['wga'], lw['bga'])
    return dict(qcat=qcat, kcat=kcat, ckv=ckv, krope=krope, qb=qb, kb=kb, vb=vb, qi=qi, kw=kw, ki=ki,
                qc=qc, kc=kc, vc=vc, la=la, gg=gg)


def _flash_update(s, v1, m_ref, acc_ref, g):
    m_prev = m_ref[g]
    m_new = jnp.maximum(m_prev, jnp.max(s, axis=1, keepdims=True))
    alpha = jnp.exp(m_prev - m_new)
    p = jnp.exp(s - jnp.concatenate([m_new] * (s.shape[1] // LANES), axis=1))
    acc_ref[g] = jnp.concatenate([alpha, alpha], axis=1) * acc_ref[g] + _dot(p.astype(BF16), v1)
    m_ref[g] = m_new


def _mla_p_kernel(q_ref, k_ref, wuv_ref, o_ref, m_ref, acc_ref, *, tq, tk, ngrp):
    qi = pl.program_id(1)
    nh = MLA_HEADS
    hpg = nh // ngrp
    qs = [jnp.concatenate([q_ref[0, :, 256 * h:256 * (h + 1)] for h in range(g * hpg, (g + 1) * hpg)], axis=0)
          for g in range(ngrp)]
    m_ref[...] = jnp.full_like(m_ref, -jnp.inf)
    acc_ref[...] = jnp.zeros_like(acc_ref)
    ones = jnp.ones((tk, LANES), BF16)

    def step(j, mask):
        start = pl.multiple_of(j * tk, tk)
        k = k_ref[0, pl.ds(start, tk), :]
        v1 = jnp.concatenate([k[:, :MLA_KV_LORA], ones], axis=1)
        for g in range(ngrp):
            s = _dot_nt(qs[g], k)
            if mask is not None:
                s = jnp.where(mask, s, -jnp.inf)
            _flash_update(s, v1, m_ref, acc_ref, g)

    nfull = qi * (tq // tk)

    def body(j, c):
        step(j, None)
        return c

    lax.fori_loop(0, nfull, body, 0)
    row = lax.broadcasted_iota(jnp.int32, (tq, tk), 0)
    col = lax.broadcasted_iota(jnp.int32, (tq, tk), 1)
    for dj in range(tq // tk):
        mk = row >= col + dj * tk
        step(nfull + dj, jnp.concatenate([mk] * hpg, axis=0))

    out = jnp.zeros((tq, 512), F32)
    for g in range(ngrp):
        acc = acc_ref[g]
        o_lat = (acc[:, :MLA_KV_LORA] / acc[:, MLA_KV_LORA:]).astype(BF16)
        for hh in range(hpg):
            out = out + _dot(o_lat[hh * tq:(hh + 1) * tq], wuv_ref[g * hpg + hh])
    o_ref[0] = out.astype(BF16)


def _mla_prompt_call(qcat, kcat, wuv_pad, tq=512, tk=256, ngrp=2):
    n, t, _ = qcat.shape
    rows = MLA_HEADS // ngrp * tq
    return pl.pallas_call(
        functools.partial(_mla_p_kernel, tq=tq, tk=tk, ngrp=ngrp),
        grid=(n, t // tq),
        in_specs=[pl.BlockSpec((1, tq, 2048), lambda b, i: (b, i, 0)),
                  pl.BlockSpec((1, t, 256), lambda b, i: (b, 0, 0)),
                  _const_spec((MLA_HEADS, 128, 512))],
        out_specs=pl.BlockSpec((1, tq, 512), lambda b, i: (b, i, 0)),
        out_shape=jax.ShapeDtypeStruct((n, t, 512), BF16),
        scratch_shapes=[pltpu.VMEM((ngrp, rows, LANES), F32), pltpu.VMEM((ngrp, rows, 2 * LANES), F32)],
        compiler_params=_cparams("parallel", "arbitrary"),
        name="mla_prompt",
    )(qcat, kcat, wuv_pad)


def _sort_key(score):
    bits = lax.bitcast_convert_type(score + 0.0, jnp.int32)
    return jnp.where(bits < 0, bits ^ jnp.int32(0x7FFFFFFF), bits)


def _lane_fold(x):
    parts = [x[:, i * LANES:(i + 1) * LANES] for i in range(x.shape[1] // LANES)]
    while len(parts) > 1:
        parts = [parts[i] + parts[i + 1] if i + 1 < len(parts) else parts[i] for i in range(0, len(parts), 2)]
    return parts[0]


def _count(mask):
    return jnp.sum(_lane_fold(jnp.where(mask, 1.0, 0.0)), axis=1, keepdims=True)


def _sublane_fold(x):
    parts = [x[i * SUBLANES:(i + 1) * SUBLANES] for i in range(x.shape[0] // SUBLANES)]
    while len(parts) > 1:
        parts = [parts[i] + parts[i + 1] if i + 1 < len(parts) else parts[i] for i in range(0, len(parts), 2)]
    return parts[0]


def _kth_and_ties(count_fn, vshape, kk, idx_bits, tie_counter=None):
    kkf = jnp.float32(kk)
    thr = jnp.where(count_fn(lambda k, i: k >= 0) >= kkf, jnp.int32(0), jnp.int32(INT_MIN))

    def vbody(b, thr):
        cand = thr | lax.shift_left(jnp.int32(1), jnp.int32(30) - b)
        return jnp.where(count_fn(lambda k, i: k >= cand) >= kkf, cand, thr)

    thr = lax.fori_loop(0, 31, vbody, thr)
    need = kkf - count_fn(lambda k, i: k > thr)
    surplus = jnp.max(count_fn(lambda k, i: k == thr) - need)

    def tie(_):
        if tie_counter is None:
            ties_below = lambda cand: count_fn(lambda k, i: (k == thr) & (i < cand))
        else:
            ties_below = tie_counter(thr)

        def pbody(b, p):
            cand = p | lax.shift_left(jnp.int32(1), jnp.int32(idx_bits - 1) - b)
            return jnp.where(ties_below(cand) <= need, cand, p)
        return lax.fori_loop(0, idx_bits, pbody, jnp.zeros(vshape, jnp.int32))

    p = lax.cond(surplus > 0.0, tie, lambda _: jnp.full(vshape, 2 ** idx_bits - 1, jnp.int32), 0)
    return thr, p


def _selected(key, pos, thr, tie_end):
    return (key > thr) | ((key == thr) & (pos < tie_end))


def _split3_q(qf, ph_ref, plo_ref):
    q_hi = qf.astype(BF16)
    q_lo = (qf - q_hi.astype(F32)).astype(BF16)
    return jnp.concatenate(
        [_dot(q_hi, ph_ref[h]) + _dot(q_lo, plo_ref[h]) for h in range(IDX_HEADS)], axis=0).astype(BF16)


def _split3_k(kk, pkh_ref, pkl_ref):
    w = kk.shape[1]
    k_hi = kk.astype(BF16)
    k_lo = (kk - k_hi.astype(F32)).astype(BF16)
    return (_dot(k_hi, pkh_ref[:w, :]) + _dot(k_lo, pkl_ref[:w, :])).astype(BF16)


def _idx_weighted(dots, wq, tq):
    score = jnp.zeros((tq, dots.shape[1]), F32)
    for h in range(IDX_HEADS):
        wh = wq[:, IDX_DIM + h:IDX_DIM + h + 1] * IDX_SCALE
        score = score + wh * jnp.maximum(dots[h * tq:(h + 1) * tq], 0.0)
    return score


def _idx_scores(q3, k3, wq, tq):
    return _idx_weighted(_dot_nt(q3, k3), wq, tq)


def _dsa_p_kernel(qi_ref, kwq_ref, kwk_ref, qb_ref, kb_ref, vb_ref, ph_ref, plo_ref, pkh_ref, pkl_ref,
                  pq_ref, po_ref, o_ref, k3_ref, kb16_ref, vb1_ref, key_ref, keyt_ref, tiep_ref, m_ref, acc_ref,
                  *, tq, ck, topk):
    it = pl.program_id(1)
    s_len = kwk_ref.shape[1]
    grp = DSA_HEADS // DSA_KV_HEADS

    @pl.when(it == 0)
    def _():
        k3_ref[...] = _split3_k(kwk_ref[0], pkh_ref, pkl_ref)
        kb16_ref[...] = kb_ref[0].astype(BF16)
        vb1_ref[...] = jnp.concatenate([vb_ref[0], jnp.ones((s_len, LANES), F32)], axis=1).astype(BF16)

    nc = lax.shift_right_logical(it * tq + (tq + ck - 1), int(ck).bit_length() - 1)
    qpos = it * tq + lax.broadcasted_iota(jnp.int32, (tq, ck), 0)
    lanes = lax.broadcasted_iota(jnp.int32, (tq, ck), 1)
    chunk_start = lambda c: pl.multiple_of(c * ck, ck)

    q3 = _split3_q(qi_ref[0], ph_ref, plo_ref)
    wq = kwq_ref[0]

    def sc_body(c, carry):
        start = chunk_start(c)
        score = _idx_scores(q3, k3_ref[pl.ds(start, ck), :], wq, tq)
        key = _sort_key(jnp.where(start + lanes <= qpos, score, -jnp.inf))
        key_ref[:, pl.ds(start, ck)] = key
        keyt_ref[pl.ds(start, ck), :] = key.T
        return carry

    lax.fori_loop(0, nc, sc_body, 0)

    kpos_t = lax.broadcasted_iota(jnp.int32, (ck, tq), 0)

    def count_fn(pred):
        def body(c, acc):
            start = chunk_start(c)
            hit = pred(keyt_ref[pl.ds(start, ck), :], start + kpos_t)
            return acc + _sublane_fold(jnp.where(hit, 1.0, 0.0))
        acc = lax.fori_loop(0, nc, body, jnp.zeros((SUBLANES, tq), F32))
        return jnp.sum(acc, axis=0, keepdims=True)

    idx_bits = int(s_len).bit_length()

    def tie_counter(thr_row):
        def fill(c, carry):
            start = chunk_start(c)
            tied = keyt_ref[pl.ds(start, ck), :] == thr_row
            tiep_ref[pl.ds(start, ck), :] = jnp.where(tied, start + kpos_t, INT_MAX)
            return carry
        lax.fori_loop(0, nc, fill, 0)

        def ties_below(cand):
            def body(c, acc):
                hit = tiep_ref[pl.ds(chunk_start(c), ck), :] < cand
                return acc + _sublane_fold(jnp.where(hit, 1.0, 0.0))
            acc = lax.fori_loop(0, nc, body, jnp.zeros((SUBLANES, tq), F32))
            return jnp.sum(acc, axis=0, keepdims=True)
        return ties_below

    def search(_):
        thr_row, tie_row = _kth_and_ties(count_fn, (1, tq), topk, idx_bits, tie_counter)
        return (jnp.broadcast_to(thr_row, (LANES, tq)).T, jnp.broadcast_to(tie_row, (LANES, tq)).T)

    thr, tie_end = lax.cond(
        it * tq + tq > topk, search,
        lambda _: (jnp.full((tq, LANES), INT_MIN, jnp.int32), jnp.full((tq, LANES), 2 ** idx_bits - 1, jnp.int32)),
        0)
    nrep = ck // LANES
    thr = jnp.concatenate([thr] * nrep, axis=1)
    tie_end = jnp.concatenate([tie_end] * nrep, axis=1)

    m_ref[...] = jnp.full_like(m_ref, MASKED)
    acc_ref[...] = jnp.zeros_like(acc_ref)
    qb = qb_ref[0]
    qrows = []
    for j in range(DSA_KV_HEADS):
        qp = _dot(qb, pq_ref[j])
        qrows.append(jnp.concatenate([qp[:, 128 * g:128 * (g + 1)] for g in range(grp)], axis=0).astype(BF16))

    def at_body(c, carry):
        start = chunk_start(c)
        kc = key_ref[:, pl.ds(start, ck)]
        pos = start + lanes
        sel = _selected(kc, pos, thr, tie_end) & (pos <= qpos)
        sel4 = jnp.concatenate([sel] * grp, axis=0)
        kb = kb16_ref[pl.ds(start, ck), :]
        v1 = vb1_ref[pl.ds(start, ck), :]
        for j in range(DSA_KV_HEADS):
            s = jnp.where(sel4, _dot_nt(qrows[j], kb), MASKED)
            _flash_update(s, v1, m_ref, acc_ref, j)
        return carry

    lax.fori_loop(0, nc, at_body, 0)
    out = jnp.zeros((tq, 512), F32)
    for j in range(DSA_KV_HEADS):
        acc = acc_ref[j]
        o = (acc[:, :LANES] / acc[:, LANES:]).astype(BF16)
        for g in range(grp):
            out = out + _dot(o[g * tq:(g + 1) * tq], po_ref[j * grp + g])
    o_ref[0] = out.astype(BF16)


def _dsa_prompt_call(pr, consts, tq=256, ck=512):
    n, t, _ = pr['qi'].shape
    ck = min(ck, t)
    tq = min(tq, t)
    assert t % ck == 0 and ck & (ck - 1) == 0
    topk = min(IDX_TOPK_MAX, t // 4)
    rows = DSA_HEADS // DSA_KV_HEADS * tq
    qspec = lambda w: pl.BlockSpec((1, tq, w), lambda b, i: (b, i, 0))
    sspec = lambda w: pl.BlockSpec((1, t, w), lambda b, i: (b, 0, 0))
    return pl.pallas_call(
        functools.partial(_dsa_p_kernel, tq=tq, ck=ck, topk=topk),
        grid=(n, t // tq),
        in_specs=[qspec(256), qspec(128), sspec(128), qspec(512), sspec(128), sspec(128),
                  _const_spec((4, 256, 256)), _const_spec((4, 256, 256)), _const_spec((128, 256)),
                  _const_spec((128, 256)), _const_spec((2, 512, 512)), _const_spec((8, 128, 512))],
        out_specs=qspec(512),
        out_shape=jax.ShapeDtypeStruct((n, t, 512), BF16),
        scratch_shapes=[pltpu.VMEM((t, 256), BF16), pltpu.VMEM((t, 128), BF16), pltpu.VMEM((t, 256), BF16),
                        pltpu.VMEM((tq, t), jnp.int32), pltpu.VMEM((t, tq), jnp.int32), pltpu.VMEM((t, tq), jnp.int32),
                        pltpu.VMEM((DSA_KV_HEADS, rows, LANES), F32), pltpu.VMEM((DSA_KV_HEADS, rows, 2 * LANES), F32)],
        compiler_params=_cparams("parallel", "arbitrary"),
        name="dsa_prompt",
    )(pr['qi'], pr['kw'], pr['kw'], pr['qb'], pr['kb'], pr['vb'],
      consts['ph'], consts['pl'], consts['pkh'], consts['pkl'], consts['pq'], consts['po'])


def _gla_kernel(q_ref, k_ref, v_ref, la_ref, s0_ref, ones_ref, o_ref, sfin_ref, st_ref, *, chunk, tt):
    ti = pl.program_id(1)
    nh, dk, dv = GLA_HEADS, GLA_DK, GLA_DV
    hd = nh * dk

    @pl.when(ti == 0)
    def _():
        st_ref[...] = s0_ref[0].astype(F32).reshape(hd, dv)

    q = q_ref[0]
    k = k_ref[0]
    v = v_ref[0]
    la = la_ref[0]
    rc = lax.broadcasted_iota(jnp.int32, (tt, hd), 0) % chunk
    rcv = lax.broadcasted_iota(jnp.int32, (tt, nh * dv), 0) % chunk

    b = la
    sft = 1
    while sft < chunk:
        b = b + jnp.where(rc >= sft, pltpu.roll(b, sft, 0), 0.0)
        sft *= 2
    bl = jnp.where(rc == chunk - 1, b, 0.0)
    sft = 1
    while sft < chunk:
        bl = bl + jnp.where(rc < chunk - sft, pltpu.roll(bl, tt - sft, 0), 0.0)
        sft *= 2

    ones = ones_ref[...]
    o_intra = jnp.zeros((tt, nh * dv), F32)
    for lag in range(chunk):
        if lag == 0:
            pr = q * k
            vs = v
        else:
            valid = rc >= lag
            ks = pltpu.roll(k, lag, 0)
            bs = pltpu.roll(b, lag, 0)
            pr = jnp.where(valid, q * ks * jnp.exp(jnp.minimum(b - bs, 0.0)), 0.0)
            vs = jnp.where(rcv >= lag, pltpu.roll(v, lag, 0), 0.0)
        att = _dot(pr.astype(BF16), ones)
        o_intra = o_intra + att * vs

    qe = q * jnp.exp(b)
    kd = k * jnp.exp(bl - b)
    pad = LANES - tt
    if pad:
        kd = jnp.concatenate([kd, jnp.zeros((pad, hd), F32)], axis=0)
        vp = jnp.concatenate([v, jnp.zeros((pad, nh * dv), F32)], axis=0)
        blp = jnp.concatenate([bl, jnp.zeros((pad, hd), F32)], axis=0)
    else:
        vp, blp = v, bl
    kdt = kd.T.astype(BF16)
    dect = jnp.exp(blp.T)
    v16 = vp.astype(BF16)
    lane = lax.broadcasted_iota(jnp.int32, (hd, LANES), 1)
    rowh = lax.broadcasted_iota(jnp.int32, (nh * chunk, hd), 0) // chunk
    colh = lax.broadcasted_iota(jnp.int32, (nh * chunk, hd), 1) // dk
    headmask = rowh == colh
    st = st_ref[...]
    outs = []
    for c in range(tt // chunk):
        qe_c = qe[c * chunk:(c + 1) * chunk]
        x = jnp.where(headmask, jnp.concatenate([qe_c] * nh, axis=0), 0.0).astype(BF16)
        r = _dot(x, st.astype(BF16))
        outs.append(jnp.concatenate([r[h * chunk:(h + 1) * chunk] for h in range(nh)], axis=1))
        inchunk = (lane >= c * chunk) & (lane < (c + 1) * chunk)
        u = _dot(jnp.where(inchunk, kdt, jnp.zeros_like(kdt)), v16)
        upd = jnp.concatenate([u[h * dk:(h + 1) * dk, h * dv:(h + 1) * dv] for h in range(nh)], axis=0)
        last = (c + 1) * chunk - 1
        st = dect[:, last:last + 1] * st + upd
    st_ref[...] = st
    o_inter = outs[0] if len(outs) == 1 else jnp.concatenate(outs, axis=0)
    o_ref[0] = o_inter + o_intra

    @pl.when(ti == pl.num_programs(1) - 1)
    def _():
        sfin_ref[0] = st.reshape(nh, dk, dv)


def _gla_call(qc, kc, vc, la, s0, ones_blk):
    n, t, _ = qc.shape
    chunk = int(np.gcd(t, GLA_CHUNK))
    tt = min(t, LANES)
    assert t % tt == 0 and tt % chunk == 0
    tspec = lambda w: pl.BlockSpec((1, tt, w), lambda b, i: (b, i, 0))
    sspec = pl.BlockSpec((1, GLA_HEADS, GLA_DK, GLA_DV), lambda b, i: (b, 0, 0, 0))
    return pl.pallas_call(
        functools.partial(_gla_kernel, chunk=chunk, tt=tt),
        grid=(n, t // tt),
        in_specs=[tspec(256), tspec(256), tspec(512), tspec(256), sspec, _const_spec((256, 512))],
        out_specs=[tspec(512), sspec],
        out_shape=[jax.ShapeDtypeStruct((n, t, 512), F32),
                   jax.ShapeDtypeStruct((n, GLA_HEADS, GLA_DK, GLA_DV), F32)],
        scratch_shapes=[pltpu.VMEM((GLA_HEADS * GLA_DK, GLA_DV), F32)],
        compiler_params=_cparams("parallel", "arbitrary"),
        name="gla",
    )(qc, kc, vc, la, s0, ones_blk)


def _mix_kernel(h_ref, x_ref, mod_ref, g_ref, oa_ref, ob_ref, oc_ref, gg_ref, ggla_ref,
                wm_ref, wbr_ref, wo_ref, o_ref):
    sb, tt, d = x_ref.shape
    tm = sb * tt
    h = h_ref[...].reshape(tm, d)
    gates = _sigmoid(_dot(h, wm_ref[...]))
    oc = oc_ref[...].reshape(tm, BRANCH_W)
    gg = gg_ref[...].reshape(tm, BRANCH_W)
    ggla = ggla_ref[...]
    parts = []
    for hh in range(GLA_HEADS):
        seg = oc[:, GLA_DV * hh:GLA_DV * (hh + 1)]
        parts.append(_rms(seg, ggla))
    ocn = (jnp.concatenate(parts, axis=1) * _silu(gg)).astype(BF16)
    ya = _dot(oa_ref[...].reshape(tm, BRANCH_W), wbr_ref[0])
    yb = _dot(ob_ref[...].reshape(tm, BRANCH_W), wbr_ref[1])
    yc = _dot(ocn, wbr_ref[2])
    y = gates[:, :d] * ya + gates[:, d:2 * d] * yb + gates[:, 2 * d:] * yc
    out = _dot(y.astype(BF16), wo_ref[...]).reshape(sb, tt, d)
    o_ref[...] = x_ref[...] + mod_ref[:, 5:6, :] * _rms(out, g_ref[3:4, :])


def _mix_call(h, x, mod, norm_g, oa, ob, oc, gg, lw, tm=256):
    n, t, d = x.shape
    sb, tt, tps = _tok_tiling(n, t, tm)
    tok = lambda w: _tok_spec(sb, tt, tps, w)
    return pl.pallas_call(
        _mix_kernel,
        grid=((n // sb) * tps,),
        in_specs=[tok(d), tok(d), _seq_spec(sb, tps, 9, d), _const_spec((6, d)),
                  tok(512), tok(512), tok(512), tok(512), _const_spec((1, 128)),
                  _const_spec((d, 3 * d)), _const_spec((3, 512, d)), _const_spec((d, d))],
        out_specs=tok(d),
        out_shape=jax.ShapeDtypeStruct((n, t, d), F32),
        compiler_params=_cparams("parallel"),
        name="mix_out",
    )(h, x, mod, norm_g, oa, ob, oc, gg, lw['g_gla'], lw['wm'], lw['wbr'], lw['wo'])


def _smp1_kernel(pt_ref, q_ref, qi_ref, kwq_ref, kcn_ref, kwn_ref, ph_ref, plo_ref, pkh_ref, pkl_ref, wuv_ref,
                 *rest, pps, t_new):
    lat = rest[:pps]
    kro = rest[pps:2 * pps]
    ikp = rest[2 * pps:3 * pps]
    o_ref, sc_ref, m_ref, l_ref, acc_ref = rest[3 * pps:]
    j = pl.program_id(1)
    nj = pl.num_programs(1)
    nh = MLA_HEADS
    tq = t_new

    @pl.when(j == 0)
    def _():
        m_ref[...] = jnp.full_like(m_ref, -jnp.inf)
        l_ref[...] = jnp.zeros_like(l_ref)
        acc_ref[...] = jnp.zeros_like(acc_ref)

    qall = q_ref[0].astype(F32)
    qa = jnp.concatenate([qall[:, 256 * h:256 * h + 128] for h in range(nh)], axis=0).astype(BF16)
    qr = jnp.concatenate([qall[:, 256 * h + 128:256 * h + 256] for h in range(nh)], axis=0).astype(BF16)
    q3 = _split3_q(qi_ref[0], ph_ref, plo_ref)
    wq = kwq_ref[0]

    def idx_scores(kslab):
        return _idx_scores(q3, _split3_k(kslab, pkh_ref, pkl_ref), wq, tq)

    def flash(s, vals):
        m_prev = m_ref[...]
        m_new = jnp.maximum(m_prev, jnp.max(s, axis=1, keepdims=True))
        alpha = jnp.exp(m_prev - m_new)
        p = jnp.exp(s - m_new)
        l_ref[...] = alpha * l_ref[...] + jnp.sum(p, axis=1, keepdims=True)
        acc_ref[...] = alpha * acc_ref[...] + _dot(p.astype(BF16), vals)
        m_ref[...] = m_new

    @pl.when(j < nj - 1)
    def _():
        c = jnp.concatenate([lat[i][...] for i in range(pps)], axis=0).astype(BF16)
        rt = jnp.concatenate([kro[i][...] for i in range(pps)], axis=1).astype(BF16)
        flash(_dot_nt(qa, c) + _dot(qr[:, :MLA_ROPE], rt), c)
        kt = jnp.concatenate([ikp[i][...] for i in range(pps)], axis=1)
        k_hi = kt.astype(BF16)
        k_lo = (kt - k_hi.astype(F32)).astype(BF16)
        k3t = jnp.concatenate([k_hi, k_lo, k_hi, jnp.zeros_like(k_hi)], axis=0)
        sc_ref[0] = _idx_weighted(_dot(q3, k3t), wq, tq)

    @pl.when(j == nj - 1)
    def _():
        kc = jnp.concatenate([kcn_ref[0].astype(F32), jnp.zeros((LANES - tq, 256), F32)], axis=0).astype(BF16)
        s = _dot_nt(jnp.concatenate([qa, qr], axis=1), kc)
        row = lax.broadcasted_iota(jnp.int32, (nh * tq, LANES), 0) % tq
        col = lax.broadcasted_iota(jnp.int32, (nh * tq, LANES), 1)
        flash(jnp.where(col <= row, s, -jnp.inf), kc[:, :MLA_KV_LORA])
        o_lat = acc_ref[...] / l_ref[...]
        out = jnp.zeros((tq, 512), F32)
        for h in range(nh):
            out = out + _dot(o_lat[h * tq:(h + 1) * tq].astype(BF16), wuv_ref[h])
        o_ref[0] = out.astype(BF16)
        kslab = jnp.concatenate([kwn_ref[0], jnp.zeros((LANES - tq, LANES), F32)], axis=0)
        sc = idx_scores(kslab)
        r8 = lax.broadcasted_iota(jnp.int32, (tq, LANES), 0)
        c8 = lax.broadcasted_iota(jnp.int32, (tq, LANES), 1)
        tail = jnp.where(c8 <= r8, sc, -jnp.inf)
        fill = jnp.full((tq, pps * PAGE_SIZE - LANES), -jnp.inf, F32)
        sc_ref[0] = jnp.concatenate([tail, fill], axis=1)


def _page_spec(layer, i, pps, n_pages, rows, cols):
    return pl.BlockSpec((None, None, rows, cols),
                        lambda b, j, pt: (layer, pt[b, jnp.minimum(j * pps + i, n_pages - pps + i)], 0, 0))


def _smp1_call(layer, pr, page_table, lat_pool, krt_pool, ikt_pool, consts, wuv_pad, pps=16):
    n, t, _ = pr['qcat'].shape
    n_pages = page_table.shape[1]
    pps = min(pps, n_pages)
    assert n_pages % pps == 0
    nj = n_pages // pps + 1
    width = n_pages * PAGE_SIZE + LANES
    page_spec = lambda i, rows, cols: _page_spec(layer, i, pps, n_pages, rows, cols)

    nspec = lambda w: pl.BlockSpec((1, t, w), lambda b, j, pt: (b, 0, 0))
    grid_spec = pltpu.PrefetchScalarGridSpec(
        num_scalar_prefetch=1,
        grid=(n, nj),
        in_specs=[nspec(2048), nspec(256), nspec(128), nspec(256), nspec(128),
                  _const_spec((4, 256, 256)), _const_spec((4, 256, 256)), _const_spec((128, 256)),
                  _const_spec((128, 256)), _const_spec((MLA_HEADS, 128, 512))]
                 + [page_spec(i, PAGE_SIZE, MLA_KV_LORA) for i in range(pps)]
                 + [page_spec(i, MLA_ROPE, PAGE_SIZE) for i in range(pps)]
                 + [page_spec(i, IDX_DIM, PAGE_SIZE) for i in range(pps)],
        out_specs=[pl.BlockSpec((1, t, 512), lambda b, j, pt: (b, 0, 0)),
                   pl.BlockSpec((1, t, pps * PAGE_SIZE), lambda b, j, pt: (b, 0, j))],
        scratch_shapes=[pltpu.VMEM((MLA_HEADS * t, 1), F32), pltpu.VMEM((MLA_HEADS * t, 1), F32),
                        pltpu.VMEM((MLA_HEADS * t, MLA_KV_LORA), F32)],
    )
    return pl.pallas_call(
        functools.partial(_smp1_kernel, pps=pps, t_new=t),
        grid_spec=grid_spec,
        out_shape=[jax.ShapeDtypeStruct((n, t, 512), BF16), jax.ShapeDtypeStruct((n, t, width), F32)],
        compiler_params=_cparams("parallel", "arbitrary"),
        name="mla_idx_sample",
    )(page_table, pr['qcat'], pr['qi'], pr['kw'], pr['kcat'], pr['kw'],
      consts['ph'], consts['pl'], consts['pkh'], consts['pkl'], wuv_pad,
      *([lat_pool] * pps), *([krt_pool] * pps), *([ikt_pool] * pps))


def _smp_topk_kernel(sc_ref, thr_ref, tie_ref, *, topk):
    sb, tq, width = sc_ref.shape
    rows = sb * tq
    key = _sort_key(sc_ref[...].reshape(rows, width))
    idx = lax.broadcasted_iota(jnp.int32, (rows, width), 1)
    thr, tie_end = _kth_and_ties(lambda pred: _count(pred(key, idx)), (rows, 1), topk, int(width).bit_length())
    thr_ref[...] = jnp.broadcast_to(thr, (rows, LANES)).reshape(sb, tq, LANES)
    tie_ref[...] = jnp.broadcast_to(tie_end, (rows, LANES)).reshape(sb, tq, LANES)


def _smp_topk_call(scores, topk, sb=16):
    n, t, width = scores.shape
    sb = min(sb, n)
    assert n % sb == 0
    spec = lambda w: pl.BlockSpec((sb, t, w), lambda i: (i, 0, 0))
    return pl.pallas_call(
        functools.partial(_smp_topk_kernel, topk=topk),
        grid=(n // sb,),
        in_specs=[spec(width)],
        out_specs=[spec(LANES), spec(LANES)],
        out_shape=[jax.ShapeDtypeStruct((n, t, LANES), jnp.int32)] * 2,
        compiler_params=_cparams("parallel"),
        name="topk_sample",
    )(scores)


def _smp2_kernel(pt_ref, sc_ref, thr_ref, tie_ref, qb_ref, kbn_ref, vbn_ref, pq_ref, po_ref, *rest, pps, t_new):
    kp = rest[:pps]
    vp = rest[pps:2 * pps]
    o_ref, q_scr, m_ref, l_ref, acc_ref = rest[2 * pps:]
    j = pl.program_id(1)
    nj = pl.num_programs(1)
    tq = t_new
    grp = DSA_HEADS // DSA_KV_HEADS
    cw = pps * PAGE_SIZE

    @pl.when(j == 0)
    def _():
        m_ref[...] = jnp.full_like(m_ref, -jnp.inf)
        l_ref[...] = jnp.zeros_like(l_ref)
        acc_ref[...] = jnp.zeros_like(acc_ref)
        qb = qb_ref[0]
        rows = []
        for jj in range(DSA_KV_HEADS):
            qp = _dot(qb, pq_ref[jj])
            rows += [qp[:, 128 * g:128 * (g + 1)] for g in range(grp)]
        q_scr[...] = jnp.concatenate(rows, axis=0).astype(BF16)

    def flash(s, pv):
        m_prev = m_ref[...]
        m_new = jnp.maximum(m_prev, jnp.max(s, axis=1, keepdims=True))
        m_safe = jnp.where(m_new == -jnp.inf, 0.0, m_new)
        alpha = jnp.exp(m_prev - m_safe)
        p = jnp.exp(s - m_safe)
        l_ref[...] = alpha * l_ref[...] + jnp.sum(p, axis=1, keepdims=True)
        acc_ref[...] = alpha * acc_ref[...] + pv(p.astype(BF16))
        m_ref[...] = m_new

    q = q_scr[...]
    sc = sc_ref[0]
    pos = j * cw + lax.broadcasted_iota(jnp.int32, (tq, cw), 1)
    thr = jnp.concatenate([thr_ref[0]] * pps, axis=1)
    tie_end = jnp.concatenate([tie_ref[0]] * pps, axis=1)
    sel = _selected(_sort_key(sc), pos, thr, tie_end) & (sc > -jnp.inf)
    msk = jnp.concatenate([sel] * (DSA_KV_HEADS * grp), axis=0)

    @pl.when(j < nj - 1)
    def _():
        kt = jnp.concatenate([kp[i][...] for i in range(pps)], axis=1).astype(BF16)
        vt = jnp.concatenate([vp[i][...] for i in range(pps)], axis=1).astype(BF16)
        flash(jnp.where(msk, _dot(q, kt), -jnp.inf), lambda p: _dot_nt(p, vt))

    @pl.when(j == nj - 1)
    def _():
        zr = jnp.zeros((LANES - tq, LANES), F32)
        k = jnp.concatenate([kbn_ref[0], zr], axis=0).astype(BF16)
        v = jnp.concatenate([vbn_ref[0], zr], axis=0).astype(BF16)
        flash(jnp.where(msk[:, :LANES], _dot_nt(q, k), -jnp.inf), lambda p: _dot(p, v))
        o = acc_ref[...] / l_ref[...]
        out = jnp.zeros((tq, 512), F32)
        for jj in range(DSA_KV_HEADS):
            for g in range(grp):
                r0 = (jj * grp + g) * tq
                out = out + _dot(o[r0:r0 + tq].astype(BF16), po_ref[jj * grp + g])
        o_ref[0] = out.astype(BF16)


def _smp2_call(layer, pr, scores, page_table, kt_pool, vt_pool, consts, pps=16):
    n, t, _ = pr['qb'].shape
    n_pages = page_table.shape[1]
    pps = min(pps, n_pages)
    nj = n_pages // pps + 1
    width = scores.shape[2]
    assert width == n_pages * PAGE_SIZE + LANES
    past = n_pages * PAGE_SIZE
    topk = min(IDX_TOPK_MAX, (past + t) // 4)
    thr, tie_end = _smp_topk_call(scores, topk)
    nrow = DSA_HEADS * t
    cw = pps * PAGE_SIZE
    page_spec = lambda i: _page_spec(layer, i, pps, n_pages, DSA_KV_HEADS * DSA_HEAD_DIM, PAGE_SIZE)

    nspec = lambda w: pl.BlockSpec((1, t, w), lambda b, j, pt: (b, 0, 0))
    grid_spec = pltpu.PrefetchScalarGridSpec(
        num_scalar_prefetch=1,
        grid=(n, nj),
        in_specs=[pl.BlockSpec((1, t, cw), lambda b, j, pt: (b, 0, j)), nspec(LANES), nspec(LANES),
                  nspec(512), nspec(128), nspec(128),
                  _const_spec((2, 512, 512)), _const_spec((8, 128, 512))]
                 + [page_spec(i) for i in range(pps)] + [page_spec(i) for i in range(pps)],
        out_specs=pl.BlockSpec((1, t, 512), lambda b, j, pt: (b, 0, 0)),
        scratch_shapes=[pltpu.VMEM((nrow, 128), BF16),
                        pltpu.VMEM((nrow, 1), F32), pltpu.VMEM((nrow, 1), F32), pltpu.VMEM((nrow, 128), F32)],
    )
    return pl.pallas_call(
        functools.partial(_smp2_kernel, pps=pps, t_new=t),
        grid_spec=grid_spec,
        out_shape=jax.ShapeDtypeStruct((n, t, 512), BF16),
        compiler_params=_cparams("parallel", "arbitrary"),
        name="dsa_sample",
    )(page_table, scores, thr, tie_end, pr['qb'], pr['kb'], pr['vb'], consts['pq'], consts['po'],
      *([kt_pool] * pps), *([vt_pool] * pps))


def _placement_consts():
    ph = np.zeros((4, 256, 256), np.float32)
    plo = np.zeros((4, 256, 256), np.float32)
    for h in range(4):
        for dd in range(64):
            ph[h, 64 * h + dd, dd] = 1.0
            ph[h, 64 * h + dd, 64 + dd] = 1.0
            plo[h, 64 * h + dd, 128 + dd] = 1.0
    pkh = np.zeros((128, 256), np.float32)
    pkl = np.zeros((128, 256), np.float32)
    for dd in range(64):
        pkh[dd, dd] = 1.0
        pkh[dd, 128 + dd] = 1.0
        pkl[dd, 64 + dd] = 1.0
    pq = np.zeros((2, 512, 512), np.float32)
    po = np.zeros((8, 128, 512), np.float32)
    for j in range(2):
        for g in range(4):
            head = 4 * j + g
            for e in range(64):
                pq[j, 64 * head + e, 128 * g + 64 * j + e] = 1.0
                po[head, 64 * j + e, 64 * head + e] = 1.0
    ones = np.zeros((256, 512), np.float32)
    for h in range(4):
        ones[64 * h:64 * (h + 1), 128 * h:128 * (h + 1)] = 1.0
    mk = lambda a: jnp.asarray(a, BF16)
    return dict(ph=mk(ph), pl=mk(plo), pkh=mk(pkh), pkl=mk(pkl), pq=mk(pq), po=mk(po), ones=mk(ones))


def _rope_tables(pos, rep):
    def base(half):
        inv = ROPE_THETA ** (-jnp.arange(half, dtype=F32) / half)
        ang = pos.astype(F32)[:, None] * inv[None, :]
        c, s = jnp.cos(ang), jnp.sin(ang)
        return jnp.concatenate([c, c], axis=1), jnp.concatenate([-s, s], axis=1)

    c32, s32 = base(16)
    c64, s64 = base(32)
    tpos = pos.shape[0]
    one = lambda w: jnp.ones((tpos, w), F32)
    zero = lambda w: jnp.zeros((tpos, w), F32)
    tabs = dict(
        cos32q=jnp.tile(c32, (1, 8)), sin32q=jnp.tile(s32, (1, 8)),
        cos32k=jnp.concatenate([c32, one(96)], axis=1), sin32k=jnp.concatenate([s32, zero(96)], axis=1),
        cos64=jnp.tile(c64, (1, 2)), sin64=jnp.tile(s64, (1, 2)),
        cos64kw=jnp.concatenate([c64, one(64)], axis=1), sin64kw=jnp.concatenate([s64, zero(64)], axis=1),
    )
    if rep > 1:
        tabs = {k: jnp.tile(v, (rep, 1)) for k, v in tabs.items()}
    return tabs


def _prep_layer(w_in, g_q, g_kv, w_uq, w_ukv, w_gla_a, b_gla_a, g_gla, w_branch, w_o):
    col = lambda nm: w_in[:, _IN_OFF[nm][0]:_IN_OFF[nm][1]]
    zc = lambda w: jnp.zeros((D_MODEL, w), w_in.dtype)
    wa = jnp.concatenate([col('mla_cq'), col('mla_ckv'), col('mla_krope'), zc(96)], axis=1)
    wb = jnp.concatenate([col('dsa_q'), col('dsa_k'), col('dsa_v'), col('idx_q'), col('idx_k'), col('idx_w'),
                          zc(60)], axis=1)
    wc = jnp.concatenate([col('gla_q'), col('gla_k'), col('gla_v'), col('gla_g'), col('gla_a'), zc(112)], axis=1)
    wm = col('merge')
    uq = w_uq.reshape(MLA_Q_LORA, MLA_HEADS, MLA_NOPE + MLA_ROPE)
    w_uq_r = jnp.concatenate([uq[:, :, :MLA_NOPE].reshape(MLA_Q_LORA, -1),
                              uq[:, :, MLA_NOPE:].reshape(MLA_Q_LORA, -1)], axis=1)
    w_uk = w_ukv[:, :, :MLA_NOPE]
    w_uv = w_ukv[:, :, MLA_NOPE:]
    wabs = jnp.zeros((MLA_HEADS * MLA_NOPE, MLA_HEADS * 256), w_in.dtype)
    wuv_pad = jnp.zeros((MLA_HEADS, MLA_KV_LORA, 512), w_in.dtype)
    sel = np.zeros((MLA_HEADS * MLA_ROPE, MLA_HEADS * 256), np.float32)
    for h in range(MLA_HEADS):
        wabs = wabs.at[MLA_NOPE * h:MLA_NOPE * (h + 1), 256 * h:256 * h + MLA_KV_LORA].set(w_uk[:, h, :].T)
        wuv_pad = wuv_pad.at[h, :, MLA_V * h:MLA_V * (h + 1)].set(w_uv[:, h, :])
        for dd in range(MLA_ROPE):
            sel[MLA_ROPE * h + dd, 256 * h + MLA_KV_LORA + dd] = 1.0
    wga = jnp.concatenate([w_gla_a, jnp.zeros((LANES - GLA_GATE_RANK, w_gla_a.shape[1]), w_gla_a.dtype)], axis=0)
    b16 = lambda a: a.astype(BF16)
    return dict(wa=b16(wa), wb=b16(wb), wc=b16(wc), wm=b16(wm), w_uq=b16(w_uq_r), wabs=b16(wabs),
                sel=jnp.asarray(sel, BF16), wuv_pad=b16(wuv_pad), wga=b16(wga),
                bga=b_gla_a.reshape(1, -1), g_q=g_q.reshape(1, -1), g_kv=g_kv.reshape(1, -1),
                g_gla=g_gla.reshape(1, -1), wbr=b16(w_branch), wo=b16(w_o))


TOKEN_TILE = 512
MIX_TILE = 256


def _trunk_layer(x, mod, ng, ffw, layer, lw, tabs, mixer_fn):
    n, t, _ = x.shape
    x, h = _ffn_call(x, mod, ng, *ffw, layer, 0, 0, True, tm=TOKEN_TILE)
    pr = _proj_calls(h, lw, tabs, tm=TOKEN_TILE)
    oa, ob, oc, st = mixer_fn(pr)
    x = _mix_call(h, x, mod, ng, oa, ob, oc, pr['gg'], lw, tm=MIX_TILE)
    x = _ffn_call(x, mod, ng, *ffw, layer, 1, 2, False, tm=TOKEN_TILE)
    kv_shape = (n, t, DSA_KV_HEADS, DSA_HEAD_DIM)
    return x, (pr['ckv'], pr['krope'], pr['kb'].reshape(kv_shape), pr['vb'].reshape(kv_shape), pr['ki'], st)


def kernel(x_prompt, x_sample, cache_mla_latent, cache_mla_krope, cache_dsa_k, cache_dsa_v, cache_dsa_idx_k,
           state_gla, page_table, c_prompt, c_sample, w_mod, b_mod, norm_g, w_ff_gate, w_ff_up, w_ff_down,
           w_in, g_q, g_kv, w_uq, w_ukv, w_gla_a, b_gla_a, g_gla, w_branch, w_o):
    depth = w_in.shape[0]
    nb, seq, _ = x_prompt.shape
    ndec, tdec, _ = x_sample.shape
    n_phys = cache_dsa_k.shape[1]
    past = page_table.shape[1] * PAGE_SIZE

    consts = _placement_consts()
    mod_all = _mod_call(jnp.concatenate([c_prompt, c_sample], axis=0), w_mod.astype(BF16), b_mod)
    ffw = (w_ff_gate.astype(BF16), w_ff_up.astype(BF16), w_ff_down.astype(BF16))
    tabs_p = _rope_tables(jnp.arange(seq), 1)
    tabs_s = _rope_tables(past + jnp.arange(tdec), max(1, TOKEN_TILE // tdec))
    kvd = DSA_KV_HEADS * DSA_HEAD_DIM
    krt_pool = jnp.transpose(cache_mla_krope, (0, 1, 3, 2))
    ikt_pool = jnp.transpose(cache_dsa_idx_k, (0, 1, 3, 2))
    kt_pool = jnp.transpose(cache_dsa_k, (0, 1, 3, 4, 2)).reshape(depth, n_phys, kvd, PAGE_SIZE)
    vt_pool = jnp.transpose(cache_dsa_v, (0, 1, 3, 4, 2)).reshape(depth, n_phys, kvd, PAGE_SIZE)
    zero_state = jnp.zeros((nb, GLA_HEADS, GLA_DK, GLA_DV), F32)

    xp, xs = x_prompt, x_sample
    new_p, new_s = [], []
    for l in range(depth):
        lw = _prep_layer(w_in[l], g_q[l], g_kv[l], w_uq[l], w_ukv[l], w_gla_a[l], b_gla_a[l], g_gla[l],
                         w_branch[l], w_o[l])
        mod_p = mod_all[l, :nb].reshape(nb, 9, D_MODEL)
        mod_s = mod_all[l, nb:].reshape(ndec, 9, D_MODEL)

        def prompt_mixer(pp):
            oa = _mla_prompt_call(pp['qcat'], pp['kcat'], lw['wuv_pad'])
            ob = _dsa_prompt_call(pp, consts)
            oc, st = _gla_call(pp['qc'], pp['kc'], pp['vc'], pp['la'], zero_state, consts['ones'])
            return oa, ob, oc, st

        def sample_mixer(ps):
            oa, scores = _smp1_call(l, ps, page_table, cache_mla_latent, krt_pool, ikt_pool, consts, lw['wuv_pad'])
            ob = _smp2_call(l, ps, scores, page_table, kt_pool, vt_pool, consts)
            oc, st = _gla_call(ps['qc'], ps['kc'], ps['vc'], ps['la'], state_gla[l], consts['ones'])
            return oa, ob, oc, st

        xp, st_p = _trunk_layer(xp, mod_p, norm_g[l], ffw, l, lw, tabs_p, prompt_mixer)
        xs, st_s = _trunk_layer(xs, mod_s, norm_g[l], ffw, l, lw, tabs_s, sample_mixer)
        new_p.append(st_p)
        new_s.append(st_s)

    lat_p, krope_p, k_p, v_p, ik_p, gla_p = [jnp.stack(a) for a in zip(*new_p)]
    lat_s, krope_s, k_s, v_s, ik_s, gla_s = [jnp.stack(a) for a in zip(*new_s)]
    return (xp, xs, lat_p, lat_s, krope_p, krope_s, k_p, k_s, v_p, v_s, ik_p, ik_s, gla_p, gla_s)
```
